```python
import jax
import jax.numpy as jnp
from jax import lax
import numpy as np

D_MODEL = 2048
BATCH = 8
SEQ = 2048
DEPTH = 2

GRID_W = 64
CTX_LEN = 256
EPS = 1e-6
NEG_INF = -1e30
ATT_HEADS = 16
ATT_KV_HEADS = 4
ATT_HEAD_DIM = 64
ATT_GROUP = ATT_HEADS // ATT_KV_HEADS
WINDOW = 128
ATT_BLOCK = 128
ROPE_THETA = 10000.0
ML_HEADS = 4
ML_DQK = 128
ML_DV = 256
ML_CHUNK = 128
GATE_SOFTCAP = 15.0
N_EXPERTS = 32
TOP_K = 4
D_FF = 1024
SWIGLU_LIMIT = 7.0
SWIGLU_ALPHA = 1.702
ATT_Q_W = ATT_HEADS * ATT_HEAD_DIM
ATT_KV_W = ATT_KV_HEADS * ATT_HEAD_DIM
ML_QK_W = ML_HEADS * ML_DQK
ML_V_W = ML_HEADS * ML_DV
N_GATES = 4 * ML_HEADS
PROJ_SIZES = (ATT_Q_W, ATT_KV_W, ATT_KV_W, ML_QK_W, ML_QK_W, ML_V_W, ML_V_W, N_GATES)
D_IN = sum(PROJ_SIZES)
D_CAT = ATT_Q_W + ML_V_W

kernel_name = 'hymba_swa_mlstm_moe_dit_prefix'


def _rmsnorm(x, w):
    xf = x.astype(jnp.float32)
    y = xf * lax.rsqrt(jnp.mean(xf * xf, axis=-1, keepdims=True) + EPS)
    return (y * w.astype(jnp.float32)).astype(x.dtype)


def _heads(t, n, d):
    return t.reshape(t.shape[:2] + (n, d))


def _split_proj(p):
    idx = np.cumsum(PROJ_SIZES)[:-1].tolist()
    return jnp.split(p, idx, axis=-1)


def _rope_tables(rows):
    row = jnp.repeat(jnp.arange(rows, dtype=jnp.float32), GRID_W)
    col = jnp.tile(jnp.arange(GRID_W, dtype=jnp.float32), rows)
    half = ATT_HEAD_DIM // 2
    inv_freq = ROPE_THETA ** (-jnp.arange(0, half, 2, dtype=jnp.float32) / half)

    def tab(p):
        ang = p[:, None] * inv_freq[None, :]
        ang = jnp.concatenate([ang, ang], axis=-1)
        return jnp.cos(ang)[:, None, :], jnp.sin(ang)[:, None, :]

    return tab(row), tab(col)


def _rot_half(x, cos, sin):
    x1, x2 = jnp.split(x, 2, axis=-1)
    return x * cos + jnp.concatenate([-x2, x1], axis=-1) * sin


def _rope_2d(x, tables):
    (cr, sr), (cc, sc) = tables
    xr, xc = jnp.split(x.astype(jnp.float32), 2, axis=-1)
    return jnp.concatenate([_rot_half(xr, cr, sr), _rot_half(xc, cc, sc)], axis=-1).astype(x.dtype)


def _latent_attention(q, k, v, kc, vc, sink):
    B, S = q.shape[:2]
    C = kc.shape[1]
    nb = S // ATT_BLOCK
    w3 = 3 * ATT_BLOCK
    qb = q.reshape(B, nb, ATT_BLOCK, ATT_KV_HEADS, ATT_GROUP, ATT_HEAD_DIM)

    def windows(t):
        tp = jnp.pad(t, ((0, 0), (WINDOW, WINDOW), (0, 0), (0, 0)))
        tp = tp.reshape(B, nb + 2, ATT_BLOCK, ATT_KV_HEADS, ATT_HEAD_DIM)
        return jnp.concatenate([tp[:, :-2], tp[:, 1:-1], tp[:, 2:]], axis=2)

    kw, vw = windows(k), windows(v)
    scale = ATT_HEAD_DIM ** -0.5
    s_loc = jnp.einsum('bnqkgd,bnwkd->bnkgqw', qb, kw).astype(jnp.float32) * scale
    r = jnp.arange(ATT_BLOCK)[:, None]
    w = jnp.arange(w3)[None, :]
    rel = w - r
    j = jnp.arange(nb)[:, None, None] * ATT_BLOCK - WINDOW + w[None]
    mask = (rel >= 0) & (rel <= 2 * WINDOW) & (j >= 0) & (j < S)
    s_loc = jnp.where(mask[None, :, None, None], s_loc, NEG_INF)
    s_ctx = jnp.einsum('bnqkgd,bckd->bnkgqc', qb, kc).astype(jnp.float32) * scale
    sink_b = jnp.broadcast_to(sink.astype(jnp.float32).reshape(1, 1, ATT_KV_HEADS, ATT_GROUP, 1, 1),
                              s_loc.shape[:-1] + (1,))
    p = jax.nn.softmax(jnp.concatenate([s_loc, s_ctx, sink_b], axis=-1), axis=-1)
    p_loc = p[..., :w3].astype(v.dtype)
    p_ctx = p[..., w3:w3 + C].astype(v.dtype)
    o = (jnp.einsum('bnkgqw,bnwkd->bnqkgd', p_loc, vw)
         + jnp.einsum('bnkgqc,bckd->bnqkgd', p_ctx, vc))
    return o.reshape(B, S, ATT_Q_W)


def _context_attention(qc, kc, vc, sink):
    B, C = qc.shape[:2]
    qg = qc.reshape(B, C, ATT_KV_HEADS, ATT_GROUP, ATT_HEAD_DIM)
    s = jnp.einsum('bqkgd,bckd->bkgqc', qg, kc).astype(jnp.float32) * ATT_HEAD_DIM ** -0.5
    sink_b = jnp.broadcast_to(sink.astype(jnp.float32).reshape(1, ATT_KV_HEADS, ATT_GROUP, 1, 1),
                              s.shape[:-1] + (1,))
    p = jax.nn.softmax(jnp.concatenate([s, sink_b], axis=-1), axis=-1)[..., :C].astype(vc.dtype)
    return jnp.einsum('bkgqc,bckd->bqkgd', p, vc).reshape(B, C, ATT_Q_W)


def _mlstm_scan(q, k, v, i_pre, f_pre, state):
    B, T, H = q.shape[:3]
    nc = T // ML_CHUNK

    def chunks(t):
        t = t.reshape((B, nc, ML_CHUNK) + t.shape[2:])
        return jnp.moveaxis(jnp.swapaxes(t, 2, 3), 1, 0)

    tril = jnp.tril(jnp.ones((ML_CHUNK, ML_CHUNK), dtype=bool))

    def step(carry, xs):
        Cm, n, m = carry
        qc, kc, vc, ic, fc = xs
        b = jnp.cumsum(jax.nn.log_sigmoid(fc), axis=-1)
        dmat = jnp.where(tril, b[..., :, None] - b[..., None, :] + ic[..., None, :], NEG_INF)
        inter = b + m[..., None]
        m_t = jnp.maximum(inter, jnp.max(dmat, axis=-1))
        w_intra = jnp.einsum('bhtd,bhsd->bhts', qc, kc) * jnp.exp(dmat - m_t[..., None])
        w_inter = jnp.exp(inter - m_t)
        num = (jnp.einsum('bhts,bhsv->bhtv', w_intra, vc)
               + w_inter[..., None] * jnp.einsum('bhvd,bhtd->bhtv', Cm, qc))
        den = jnp.sum(w_intra, axis=-1) + w_inter * jnp.einsum('bhd,bhtd->bht', n, qc)
        h = num / jnp.maximum(jnp.abs(den), jnp.exp(-m_t))[..., None]
        b_last = b[..., -1]
        g = b_last[..., None] - b + ic
        m_new = jnp.maximum(b_last + m, jnp.max(g, axis=-1))
        w_s = jnp.exp(g - m_new[..., None])
        w_c = jnp.exp(b_last + m - m_new)
        C_new = w_c[..., None, None] * Cm + jnp.einsum('bhs,bhsv,bhsd->bhvd', w_s, vc, kc)
        n_new = w_c[..., None] * n + jnp.einsum('bhs,bhsd->bhd', w_s, kc)
        return (C_new, n_new, m_new), h

    state, h = lax.scan(step, state, (chunks(q), chunks(k), chunks(v), chunks(i_pre), chunks(f_pre)))
    h = jnp.swapaxes(jnp.moveaxis(h, 0, 1), 2, 3).reshape(B, T, H, ML_DV)
    return h, state


def _flip(t):
    return jnp.flip(t, axis=1)


def _mlstm_bwd(q, k, v, i_pre, f_pre, state):
    h, st = _mlstm_scan(_flip(q), _flip(k), _flip(v), _flip(i_pre), _flip(f_pre), state)
    return _flip(h), st


def _mlstm_inputs(mq, mk, mv, g, b_gates):
    q = _heads(mq, ML_HEADS, ML_DQK).astype(jnp.float32)
    k = _heads(mk, ML_HEADS, ML_DQK).astype(jnp.float32) * ML_DQK ** -0.5
    v = _heads(mv, ML_HEADS, ML_DV).astype(jnp.float32)
    g = g.astype(jnp.float32) + b_gates.astype(jnp.float32)
    g = GATE_SOFTCAP * jnp.tanh(g / GATE_SOFTCAP)
    i_f, f_f, i_b, f_b = jnp.split(g, 4, axis=-1)
    return q, k, v, i_f, f_f, i_b, f_b


def _mlstm_out(h_f, h_b, o_pre, norm_w, dtype):
    B, T = h_f.shape[:2]
    h = _rmsnorm(h_f + h_b, norm_w.reshape(ML_HEADS, ML_DV)).reshape(B, T, ML_V_W).astype(dtype)
    return h * jax.nn.sigmoid(o_pre)


def _moe(h, w_router, b_router, w_gate_up, b_gate_up, w_down, b_down):
    logits = (h @ w_router).astype(jnp.float32) + b_router.astype(jnp.float32)
    top_v, top_i = lax.top_k(logits, TOP_K)
    top_w = jax.nn.softmax(top_v, axis=-1)
    combine = jnp.sum(jax.nn.one_hot(top_i, N_EXPERTS, dtype=jnp.float32) * top_w[..., None], axis=1)
    out = jnp.zeros(h.shape, jnp.float32)
    for e in range(N_EXPERTS):
        gate, up = jnp.split(h @ w_gate_up[e] + b_gate_up[e], 2, axis=-1)
        gate = jnp.minimum(gate, SWIGLU_LIMIT)
        up = jnp.clip(up, -SWIGLU_LIMIT, SWIGLU_LIMIT)
        y = ((up + 1) * gate * jax.nn.sigmoid(SWIGLU_ALPHA * gate)) @ w_down[e] + b_down[e]
        out = out + combine[:, e:e + 1] * y
    return out.astype(h.dtype)


def _layer(x, ctx, mod_x, mod_c, norm1_w, norm2_w, w_in, b_gates, q_norm_w, k_norm_w, attn_sink,
           mlstm_norm_w, w_out, w_router, b_router, w_gate_up, b_gate_up, w_down, b_down, rope, update_ctx):
    B, S, D = x.shape
    C = ctx.shape[1]
    sh1, sc1, g1, sh2, sc2, g2 = jnp.split(mod_x, 6, axis=-1)
    csh1, csc1, cg1, csh2, csc2, cg2 = jnp.split(mod_c, 6, axis=-1)

    hx = _rmsnorm(x, norm1_w) * (1 + sc1[:, None]) + sh1[:, None]
    hc = _rmsnorm(ctx, norm1_w) * (1 + csc1) + csh1
    qx, kx, vx, mqx, mkx, mvx, mox, gx = _split_proj(hx @ w_in)
    qc, kc, vc, mqc, mkc, mvc, moc, gc = _split_proj(hc @ w_in)

    qx = _rope_2d(_rmsnorm(_heads(qx, ATT_HEADS, ATT_HEAD_DIM), q_norm_w), rope)
    kx = _rope_2d(_rmsnorm(_heads(kx, ATT_KV_HEADS, ATT_HEAD_DIM), k_norm_w), rope)
    vx = _heads(vx, ATT_KV_HEADS, ATT_HEAD_DIM)
    qc = _rmsnorm(_heads(qc, ATT_HEADS, ATT_HEAD_DIM), q_norm_w)
    kc = _rmsnorm(_heads(kc, ATT_KV_HEADS, ATT_HEAD_DIM), k_norm_w)
    vc = _heads(vc, ATT_KV_HEADS, ATT_HEAD_DIM)
    att_x = _latent_attention(qx, kx, vx, kc, vc, attn_sink)

    cq, ck, cv, ci_f, cf_f, ci_b, cf_b = _mlstm_inputs(mqc, mkc, mvc, gc, b_gates)
    lq, lk, lv, li_f, lf_f, li_b, lf_b = _mlstm_inputs(mqx, mkx, mvx, gx, b_gates)
    zero_state = (jnp.zeros((B, ML_HEADS, ML_DV, ML_DQK), jnp.float32),
                  jnp.zeros((B, ML_HEADS, ML_DQK), jnp.float32),
                  jnp.zeros((B, ML_HEADS), jnp.float32))
    hc_f, st_f = _mlstm_scan(cq, ck, cv, ci_f, cf_f, zero_state)
    hc_b, st_b = _mlstm_bwd(cq, ck, cv, ci_b, cf_b, zero_state)
    hx_f, _ = _mlstm_scan(lq, lk, lv, li_f, lf_f, st_f)
    hx_b, _ = _mlstm_bwd(lq, lk, lv, li_b, lf_b, st_b)
    ml_x = _mlstm_out(hx_f, hx_b, mox, mlstm_norm_w, x.dtype)

    x = x + g1[:, None] * (jnp.concatenate([att_x, ml_x], axis=-1) @ w_out)
    if update_ctx:
        att_c = _context_attention(qc, kc, vc, attn_sink)
        ml_c = _mlstm_out(hc_f, hc_b, moc, mlstm_norm_w, ctx.dtype)
        ctx = ctx + cg1 * (jnp.concatenate([att_c, ml_c], axis=-1) @ w_out)

    fx = (_rmsnorm(x, norm2_w) * (1 + sc2[:, None]) + sh2[:, None]).reshape(B * S, D)
    if update_ctx:
        fc = (_rmsnorm(ctx, norm2_w) * (1 + csc2) + csh2).reshape(B * C, D)
        y = _moe(jnp.concatenate([fx, fc], axis=0), w_router, b_router, w_gate_up, b_gate_up, w_down, b_down)
        x = x + g2[:, None] * y[:B * S].reshape(B, S, D)
        ctx = ctx + cg2 * y[B * S:].reshape(B, C, D)
    else:
        y = _moe(fx, w_router, b_router, w_gate_up, b_gate_up, w_down, b_down)
        x = x + g2[:, None] * y.reshape(B, S, D)
    return x, ctx


def setup_inputs(seed: int = 0) -> dict:
    key = jax.random.key(seed)
    ks = jax.random.split(key, 24)
    L, D, E, F = DEPTH, D_MODEL, N_EXPERTS, D_FF
    nrm = jax.random.normal
    f_bias = jnp.broadcast_to(jnp.linspace(3.0, 6.0, ML_HEADS, dtype=jnp.float32), (L, ML_HEADS))
    gate_noise = 0.1 * nrm(ks[4], (L, N_GATES), jnp.float32)
    b_gates = gate_noise + jnp.concatenate([jnp.zeros((L, ML_HEADS)), f_bias, jnp.zeros((L, ML_HEADS)), f_bias], axis=-1)
    return {
        'x': nrm(ks[0], (BATCH, SEQ, D), jnp.float32),
        'c': nrm(ks[1], (BATCH, D), jnp.float32),
        'ctx': nrm(ks[2], (BATCH, CTX_LEN, D), jnp.float32),
        'c_ctx': nrm(ks[3], (D,), jnp.float32),
        'w_ada': 0.5 * D ** -0.5 * nrm(ks[5], (L, D, 6 * D), jnp.float32),
        'b_ada': 0.01 * nrm(ks[6], (L, 6 * D), jnp.float32),
        'norm1_w': 1.0 + 0.1 * nrm(ks[7], (L, D), jnp.float32),
        'norm2_w': 1.0 + 0.1 * nrm(ks[8], (L, D), jnp.float32),
        'w_in': D ** -0.5 * nrm(ks[9], (L, D, D_IN), jnp.float32),
        'b_gates': b_gates,
        'q_norm_w': 1.0 + 0.1 * nrm(ks[10], (L, ATT_HEAD_DIM), jnp.float32),
        'k_norm_w': 1.0 + 0.1 * nrm(ks[11], (L, ATT_HEAD_DIM), jnp.float32),
        'attn_sink': 0.5 * nrm(ks[12], (L, ATT_HEADS), jnp.float32),
        'mlstm_norm_w': 1.0 + 0.1 * nrm(ks[13], (L, ML_V_W), jnp.float32),
        'w_out': D_CAT ** -0.5 * nrm(ks[14], (L, D_CAT, D), jnp.float32),
        'w_router': D ** -0.5 * nrm(ks[15], (L, D, E), jnp.float32),
        'b_router': 0.01 * nrm(ks[16], (L, E), jnp.float32),
        'w_gate_up': D ** -0.5 * nrm(ks[17], (L, E, D, 2 * F), jnp.float32),
        'b_gate_up': 0.01 * nrm(ks[18], (L, E, 2 * F), jnp.float32),
        'w_down': F ** -0.5 * nrm(ks[19], (L, E, F, D), jnp.float32),
        'b_down': 0.01 * nrm(ks[20], (L, E, D), jnp.float32),
    }


def reference(x, c, ctx, c_ctx, w_ada, b_ada, norm1_w, norm2_w, w_in, b_gates, q_norm_w, k_norm_w,
              attn_sink, mlstm_norm_w, w_out, w_router, b_router, w_gate_up, b_gate_up, w_down, b_down):
    ROWS = x.shape[1] // GRID_W
    rope = _rope_tables(ROWS)
    silu_c = jax.nn.silu(c)
    silu_cc = jax.nn.silu(c_ctx)
    for l in range(DEPTH):
        mod_x = silu_c @ w_ada[l] + b_ada[l]
        mod_c = silu_cc @ w_ada[l] + b_ada[l]
        x, ctx = _layer(x, ctx, mod_x, mod_c, norm1_w[l], norm2_w[l], w_in[l], b_gates[l], q_norm_w[l],
                        k_norm_w[l], attn_sink[l], mlstm_norm_w[l], w_out[l], w_router[l], b_router[l],
                        w_gate_up[l], b_gate_up[l], w_down[l], b_down[l], rope, l < DEPTH - 1)
    return x
```

```python
import functools

import numpy as np
import jax
import jax.numpy as jnp
from jax import lax
from jax.experimental import pallas as pl
from jax.experimental.pallas import tpu as pltpu

F32 = jnp.float32
BF16 = jnp.bfloat16

GRID_W = 64
EPS = 1e-6
NEG_INF = -1e30
ATT_HEADS = 16
ATT_KV_HEADS = 4
ATT_HEAD_DIM = 64
ATT_GROUP = ATT_HEADS // ATT_KV_HEADS
WINDOW = 128
ATT_BLOCK = 128
ROPE_THETA = 10000.0
ML_HEADS = 4
ML_DQK = 128
ML_DV = 256
ML_CHUNK = 128
GATE_SOFTCAP = 15.0
TOP_K = 4
SWIGLU_LIMIT = 7.0
SWIGLU_ALPHA = 1.702

ATT_Q_W = ATT_HEADS * ATT_HEAD_DIM
ATT_KV_W = ATT_KV_HEADS * ATT_HEAD_DIM
ML_QK_W = ML_HEADS * ML_DQK
ML_V_W = ML_HEADS * ML_DV
N_GATES = 4 * ML_HEADS

COL_Q = 0
COL_MV = COL_Q + ATT_Q_W
COL_MO = COL_MV + ML_V_W
COL_MQ = COL_MO + ML_V_W
COL_MK = COL_MQ + ML_QK_W
COL_K = COL_MK + ML_QK_W
COL_V = COL_K + ATT_KV_W
PROJ_W = COL_V + ATT_KV_W
QK_W = ATT_Q_W + ATT_KV_W

LANES = 128
VMEM_LIMIT = 56 * 1024 * 1024

MOE_TM = 256
TOK_TM = 256


def _dot(a, b):
    return jnp.dot(a, b, preferred_element_type=F32)


def _dot_nt(a, b):
    return lax.dot_general(a, b, (((1,), (1,)), ((), ())), preferred_element_type=F32)


def _split2(a):
    hi = a.astype(BF16)
    lo = (a - hi.astype(F32)).astype(BF16)
    return hi, lo


def _cparams(sem):
    return pltpu.CompilerParams(dimension_semantics=sem, vmem_limit_bytes=VMEM_LIMIT)


def _ada_kernel(cv_ref, w_ref, b_ref, o_ref):
    cv = cv_ref[...]
    s = cv * jax.nn.sigmoid(cv)
    o_ref[0] = _dot(s.astype(BF16), w_ref[0].astype(BF16)) + b_ref[0]


def _ada_mod(cv, w_ada, b_ada):
    L, D, N = w_ada.shape
    tn = 1024
    return pl.pallas_call(
        _ada_kernel,
        grid=(L, N // tn),
        in_specs=[pl.BlockSpec((16, D), lambda l, j: (0, 0)),
                  pl.BlockSpec((1, D, tn), lambda l, j: (l, 0, j)),
                  pl.BlockSpec((1, 1, tn), lambda l, j: (l, 0, j))],
        out_specs=pl.BlockSpec((1, 16, tn), lambda l, j: (l, 0, j)),
        out_shape=jax.ShapeDtypeStruct((L, 16, N), F32),
        compiler_params=_cparams(("arbitrary", "arbitrary")),
    )(cv, w_ada, b_ada.reshape(L, 1, N))


def _inproj_kernel(x_ref, sc_ref, sh_ref, nw_ref, w_ref, wg_ref, wgt_ref, p_ref, g_ref, gt_ref, hn_ref):
    @pl.when(pl.program_id(1) == 0)
    def _():
        x = x_ref[...]
        ms = jnp.mean(x * x, axis=-1, keepdims=True)
        y = x * lax.rsqrt(ms + EPS) * nw_ref[...]
        hb = (y * (1.0 + sc_ref[0]) + sh_ref[0]).astype(BF16)
        hn_ref[...] = hb
        g_ref[...] = _dot(hb, wg_ref[...])
        gt_ref[...] = _dot_nt(wgt_ref[...], hb)

    p_ref[...] = _dot(hn_ref[...], w_ref[...]).astype(BF16)


def _inproj(xall, mod3, norm_w, w_main, w_g, w_gt, *, n_lat_rows, seq, nbatch):
    R, D = xall.shape
    tm, tn = 1024, 1536
    n_lat_tiles = n_lat_rows // tm
    per_b = seq // tm

    def mod_idx(which):
        return lambda i, j: (jnp.where(i < n_lat_tiles, i // per_b, nbatch) * 6 + which, 0, 0)

    return pl.pallas_call(
        _inproj_kernel,
        grid=(R // tm, PROJ_W // tn),
        in_specs=[pl.BlockSpec((tm, D), lambda i, j: (i, 0)),
                  pl.BlockSpec((1, 1, D), mod_idx(1)),
                  pl.BlockSpec((1, 1, D), mod_idx(0)),
                  pl.BlockSpec((1, D), lambda i, j: (0, 0)),
                  pl.BlockSpec((D, tn), lambda i, j: (0, j)),
                  pl.BlockSpec((D, LANES), lambda i, j: (0, 0)),
                  pl.BlockSpec((N_GATES, D), lambda i, j: (0, 0))],
        out_specs=[pl.BlockSpec((tm, tn), lambda i, j: (i, j)),
                   pl.BlockSpec((tm, LANES), lambda i, j: (i, 0)),
                   pl.BlockSpec((N_GATES, tm), lambda i, j: (0, i))],
        out_shape=[jax.ShapeDtypeStruct((R, PROJ_W), BF16),
                   jax.ShapeDtypeStruct((R, LANES), F32),
                   jax.ShapeDtypeStruct((N_GATES, R), F32)],
        scratch_shapes=[pltpu.VMEM((tm, D), BF16)],
        compiler_params=_cparams(("arbitrary", "arbitrary")),
    )(xall, mod3, mod3, norm_w.reshape(1, D), w_main, w_g, w_gt)


def _norm_rope(x, w_row, cos, sa, sb, bd, scale):
    x = x.astype(F32)
    hi, lo = _split2(x * x)
    ss = _dot(hi, bd) + _dot(lo, bd)
    y = x * lax.rsqrt(ss * (1.0 / ATT_HEAD_DIM) + EPS) * w_row
    w = y.shape[-1]
    r = y * cos + pltpu.roll(y, 16, 1) * sa + pltpu.roll(y, w - 16, 1) * sb
    if scale != 1.0:
        r = r * scale
    return r.astype(BF16)


def _qkrope_kernel(q_ref, k_ref, cos_ref, sa_ref, sb_ref, qw_ref, kw_ref, bd_ref, o_ref):
    cos, sa, sb, bd = cos_ref[...], sa_ref[...], sb_ref[...], bd_ref[...]
    cw = cos.shape[-1]
    for c in range(ATT_Q_W // cw):
        o_ref[:, c * cw:(c + 1) * cw] = _norm_rope(q_ref[:, c * cw:(c + 1) * cw], qw_ref[...], cos, sa, sb, bd,
                                                   ATT_HEAD_DIM ** -0.5)
    o_ref[:, ATT_Q_W:] = _norm_rope(k_ref[...], kw_ref[...], cos, sa, sb, bd, 1.0)


def _qkrope(proj, tabs, qw, kw, bd, *, n_lat_rows, seq):
    R = proj.shape[0]
    tm = 512
    n_lat_tiles = n_lat_rows // tm
    per_b = seq // tm
    cw = ATT_KV_W
    tab_spec = pl.BlockSpec((tm, cw), lambda i: (jnp.where(i < n_lat_tiles, i % per_b, per_b), 0))
    const = lambda i: (0, 0)
    return pl.pallas_call(
        _qkrope_kernel,
        grid=(R // tm,),
        in_specs=[pl.BlockSpec((tm, ATT_Q_W), lambda i: (i, 0)),
                  pl.BlockSpec((tm, ATT_KV_W), lambda i: (i, COL_K // ATT_KV_W)),
                  tab_spec, tab_spec, tab_spec,
                  pl.BlockSpec((1, cw), const), pl.BlockSpec((1, cw), const), pl.BlockSpec((cw, cw), const)],
        out_specs=pl.BlockSpec((tm, QK_W), lambda i: (i, 0)),
        out_shape=jax.ShapeDtypeStruct((R, QK_W), BF16),
        compiler_params=_cparams(("arbitrary",)),
    )(proj, proj, tabs[0], tabs[1], tabs[2], qw, kw, bd)


def _rope_tables(seq, tm):
    rows = seq // GRID_W
    row = jnp.repeat(jnp.arange(rows, dtype=F32), GRID_W)
    col = jnp.tile(jnp.arange(GRID_W, dtype=F32), rows)
    half = ATT_HEAD_DIM // 2
    inv_freq = ROPE_THETA ** (-jnp.arange(0, half, 2, dtype=F32) / half)

    def ang(p):
        a = p[:, None] * inv_freq[None, :]
        return jnp.concatenate([a, a], axis=-1)

    a = jnp.concatenate([ang(row), ang(col)], axis=-1)
    cos, sin = jnp.cos(a), jnp.sin(a)
    second = (jnp.arange(ATT_HEAD_DIM) % half) >= (half // 2)
    sa = jnp.where(second[None, :], sin, 0.0)
    sb = jnp.where(second[None, :], 0.0, -sin)
    reps = ATT_KV_W // ATT_HEAD_DIM

    def fin(t, fill):
        t = jnp.tile(t, (1, reps))
        return jnp.concatenate([t, jnp.full((tm, t.shape[1]), fill, F32)], axis=0)

    return fin(cos, 1.0), fin(sa, 0.0), fin(sb, 0.0)


def _attn_kernel(sink_ref, q_ref, kp_ref, kc_ref, kn_ref, vp_ref, vc_ref, vn_ref, kx_ref, vx_ref, o_ref, *, n_lat):
    n = pl.program_id(1)
    blk, hd, grp = ATT_BLOCK, ATT_HEAD_DIM, ATT_GROUP
    q = q_ref[...]
    kw = jnp.concatenate([kp_ref[...], kc_ref[...], kn_ref[...]], axis=0)
    vw = jnp.concatenate([vp_ref[...], vc_ref[...], vn_ref[...]], axis=0)
    kx, vx = kx_ref[...], vx_ref[...]
    r = lax.broadcasted_iota(jnp.int32, (grp * blk, 3 * blk), 0) % blk
    w = lax.broadcasted_iota(jnp.int32, (grp * blk, 3 * blk), 1)
    rel = w - r
    jabs = n * blk - WINDOW + w
    mask = (rel >= 0) & (rel <= 2 * WINDOW) & (jabs >= 0) & (jabs < n_lat * blk) & (n < n_lat)
    for h in range(ATT_KV_HEADS):
        c0 = h * grp * hd
        qh = jnp.concatenate([q[:, c0 + g * hd:c0 + (g + 1) * hd] for g in range(grp)], axis=0)
        s_loc = jnp.where(mask, _dot_nt(qh, kw[:, h * hd:(h + 1) * hd]), NEG_INF)
        s_ctx = _dot_nt(qh, kx[:, h * hd:(h + 1) * hd])
        sink = jnp.concatenate([jnp.full((blk, 1), sink_ref[h * grp + g], F32) for g in range(grp)], axis=0)
        m = jnp.maximum(jnp.maximum(jnp.max(s_loc, axis=-1, keepdims=True),
                                    jnp.max(s_ctx, axis=-1, keepdims=True)), sink)
        p_loc = jnp.exp(s_loc - m)
        p_ctx = jnp.exp(s_ctx - m)
        den = (jnp.sum(p_loc, axis=-1, keepdims=True) + jnp.sum(p_ctx, axis=-1, keepdims=True)
               + jnp.exp(sink - m))
        o = (_dot(p_loc.astype(BF16), vw[:, h * hd:(h + 1) * hd])
             + _dot(p_ctx.astype(BF16), vx[:, h * hd:(h + 1) * hd])) / den
        o_ref[:, c0:c0 + grp * hd] = jnp.concatenate(
            [o[g * blk:(g + 1) * blk] for g in range(grp)], axis=1).astype(BF16)


def _attention(qk, proj, sink, *, nbatch, seq, ctx_len, with_ctx_queries):
    blk = ATT_BLOCK
    n_lat = seq // blk
    n_ctx = ctx_len // blk
    nb = n_lat + (n_ctx if with_ctx_queries else 0)
    lat_blocks = nbatch * n_lat
    out_rows = nbatch * nb * blk
    kcol = ATT_Q_W // ATT_KV_W
    vcol = COL_V // ATT_KV_W

    def qrow(b, n):
        return jnp.where(n < n_lat, b * n_lat + n, lat_blocks + b * n_ctx + (n - n_lat))

    def win(off):
        return lambda b, n: b * n_lat + jnp.clip(n + off, 0, n_lat - 1)

    def spec_k(off):
        f = win(off)
        return pl.BlockSpec((blk, ATT_KV_W), lambda b, n: (f(b, n), kcol))

    def spec_v(off):
        f = win(off)
        return pl.BlockSpec((blk, ATT_KV_W), lambda b, n: (f(b, n), vcol))

    ctx_row = lambda b: (nbatch * seq) // ctx_len + b
    return pl.pallas_call(
        functools.partial(_attn_kernel, n_lat=n_lat),
        grid=(nbatch, nb),
        in_specs=[pl.BlockSpec(memory_space=pltpu.SMEM),
                  pl.BlockSpec((blk, ATT_Q_W), lambda b, n: (qrow(b, n), 0)),
                  spec_k(-1), spec_k(0), spec_k(1), spec_v(-1), spec_v(0), spec_v(1),
                  pl.BlockSpec((ctx_len, ATT_KV_W), lambda b, n: (ctx_row(b), kcol)),
                  pl.BlockSpec((ctx_len, ATT_KV_W), lambda b, n: (ctx_row(b), vcol))],
        out_specs=pl.BlockSpec((blk, ATT_Q_W), lambda b, n: (qrow(b, n), 0)),
        out_shape=jax.ShapeDtypeStruct((out_rows, ATT_Q_W), BF16),
        compiler_params=_cparams(("arbitrary", "arbitrary")),
    )(sink, qk, qk, qk, qk, proj, proj, proj, qk, proj)


def _softcap(g):
    return GATE_SOFTCAP * jnp.tanh(g * (1.0 / GATE_SOFTCAP))


def _log_sigmoid(x):
    return jnp.minimum(x, 0.0) - jnp.log(1.0 + jnp.exp(-jnp.abs(x)))


def _mlstm_kernel(bg_ref, bgt_ref, gf_ref, gtf_ref, gb_ref, gtb_ref, qf_ref, kf_ref, vf_ref, qb_ref, kb_ref, vb_ref,
                  hf_ref, hb_ref, c_ref, n_ref, m_ref):
    L = ML_CHUNK

    @pl.when(pl.program_id(1) == 0)
    def _():
        c_ref[...] = jnp.zeros_like(c_ref)
        n_ref[...] = jnp.zeros_like(n_ref)
        m_ref[...] = jnp.zeros_like(m_ref)

    ri = lax.broadcasted_iota(jnp.int32, (L, L), 0)
    ci = lax.broadcasted_iota(jnp.int32, (L, L), 1)
    lower = ci <= ri
    upper = ci >= ri
    lower_b = lower.astype(BF16)
    upper_b = upper.astype(BF16)
    scale = ML_DQK ** -0.5

    dirs = ((gf_ref, gtf_ref, qf_ref, kf_ref, vf_ref, hf_ref, lower, lower_b, upper_b, L - 1),
            (gb_ref, gtb_ref, qb_ref, kb_ref, vb_ref, hb_ref, upper, upper_b, lower_b, 0))
    for d, (g_ref, gt_ref, q_ref, k_ref, v_ref, h_ref, tri, tri_b, trit_b, last) in enumerate(dirs):
        g = _softcap(g_ref[...] + bg_ref[...])
        gt = _softcap(gt_ref[...] + bgt_ref[...])
        lf_hi, lf_lo = _split2(_log_sigmoid(g))
        lft_hi, lft_lo = _split2(_log_sigmoid(gt))
        bcol_all = _dot(tri_b, lf_hi) + _dot(tri_b, lf_lo)
        brow_all = _dot(lft_hi, trit_b) + _dot(lft_lo, trit_b)
        for h in range(ML_HEADS):
            r = d * ML_HEADS + h
            ic = d * 2 * ML_HEADS + h
            fc = ic + ML_HEADS
            i_col, b_col = g[:, ic:ic + 1], bcol_all[:, fc:fc + 1]
            i_row, b_row = gt[ic:ic + 1, :], brow_all[fc:fc + 1, :]
            total = b_col[last:last + 1, :]
            m_prev = m_ref[r:r + 1, 0:1]
            inter = b_col + m_prev
            dmat = jnp.where(tri, b_col - b_row + i_row, NEG_INF)
            m_t = jnp.maximum(inter, jnp.max(dmat, axis=-1, keepdims=True))
            q = q_ref[:, h * ML_DQK:(h + 1) * ML_DQK]
            k_s = k_ref[:, h * ML_DQK:(h + 1) * ML_DQK].astype(F32) * scale
            v = v_ref[:, h * ML_DV:(h + 1) * ML_DV]
            w_intra = _dot_nt(q, k_s.astype(BF16)) * jnp.exp(dmat - m_t)
            w_inter = jnp.exp(inter - m_t)
            c_t = c_ref[r]
            n_row = n_ref[r:r + 1, :]
            num = _dot(w_intra.astype(BF16), v) + w_inter * _dot(q, c_t.astype(BF16))
            qn = jnp.sum(q.astype(F32) * n_row, axis=-1, keepdims=True)
            den = jnp.sum(w_intra, axis=-1, keepdims=True) + w_inter * qn
            h_out = num / jnp.maximum(jnp.abs(den), jnp.exp(-m_t))
            h_ref[:, h * ML_DV:(h + 1) * ML_DV] = h_out.astype(BF16)
            g_row = total - b_row + i_row
            g_col = total - b_col + i_col
            m_new = jnp.maximum(total + m_prev, jnp.max(g_row, axis=-1, keepdims=True))
            ws_col = jnp.exp(g_col - m_new)
            w_c = jnp.exp(total + m_prev - m_new)
            c_ref[r] = w_c * c_t + _dot(k_s.T.astype(BF16), (ws_col * v.astype(F32)).astype(BF16))
            n_ref[r:r + 1, :] = w_c * n_row + jnp.sum(ws_col * k_s, axis=0, keepdims=True)
            m_ref[r:r + 1, :] = jnp.broadcast_to(m_new, (1, LANES))


def _mlstm(proj, g, gt, b_gates, *, nbatch, seq, ctx_len):
    L = ML_CHUNK
    R = proj.shape[0]
    n_lat = seq // L
    n_ctx = ctx_len // L
    steps = n_ctx + n_lat
    lat_blocks = nbatch * n_lat

    def fwd(b, t):
        return jnp.where(t < n_ctx, lat_blocks + b * n_ctx + t, b * n_lat + t - n_ctx)

    def bwd(b, t):
        return jnp.where(t < n_ctx, lat_blocks + b * n_ctx + (n_ctx - 1 - t), b * n_lat + (steps - 1 - t))

    def rows(f, width, colblk):
        return pl.BlockSpec((L, width), lambda b, t: (f(b, t), colblk))

    def gt_spec(f):
        return pl.BlockSpec((N_GATES, L), lambda b, t: (0, f(b, t)))

    bg = jnp.zeros((1, LANES), F32).at[0, :N_GATES].set(b_gates)
    bgt = b_gates.reshape(N_GATES, 1)
    const = lambda b, t: (0, 0)
    qc, kc, vc = COL_MQ // ML_QK_W, COL_MK // ML_QK_W, COL_MV // ML_V_W
    return pl.pallas_call(
        _mlstm_kernel,
        grid=(nbatch, steps),
        in_specs=[pl.BlockSpec((1, LANES), const), pl.BlockSpec((N_GATES, 1), const),
                  rows(fwd, LANES, 0), gt_spec(fwd), rows(bwd, LANES, 0), gt_spec(bwd),
                  rows(fwd, ML_QK_W, qc), rows(fwd, ML_QK_W, kc), rows(fwd, ML_V_W, vc),
                  rows(bwd, ML_QK_W, qc), rows(bwd, ML_QK_W, kc), rows(bwd, ML_V_W, vc)],
        out_specs=[rows(fwd, ML_V_W, 0), rows(bwd, ML_V_W, 0)],
        out_shape=[jax.ShapeDtypeStruct((R, ML_V_W), BF16), jax.ShapeDtypeStruct((R, ML_V_W), BF16)],
        scratch_shapes=[pltpu.VMEM((2 * ML_HEADS, ML_DQK, ML_DV), F32),
                        pltpu.VMEM((2 * ML_HEADS, ML_DQK), F32),
                        pltpu.VMEM((2 * ML_HEADS, LANES), F32)],
        compiler_params=_cparams(("arbitrary", "arbitrary")),
    )(bg, bgt, g, gt, g, gt, proj, proj, proj, proj, proj, proj)


def _outproj_kernel(att_ref, hf_ref, hb_ref, mo_ref, x_ref, g1_ref, sc2_ref, sh2_ref, mnw_ref, n2w_ref,
                    wo_ref, wrh_ref, wrl_ref, br_ref, xo_ref, fx_ref, lt_ref):
    hsum = hf_ref[...].astype(F32) + hb_ref[...].astype(F32)
    parts = []
    for h in range(ML_HEADS):
        hh = hsum[:, h * ML_DV:(h + 1) * ML_DV]
        ms = jnp.mean(hh * hh, axis=-1, keepdims=True)
        parts.append(hh * lax.rsqrt(ms + EPS))
    ml = jnp.concatenate(parts, axis=1) * mnw_ref[...] * jax.nn.sigmoid(mo_ref[...].astype(F32))
    y = _dot(att_ref[...], wo_ref[:ATT_Q_W, :]) + _dot(ml.astype(BF16), wo_ref[ATT_Q_W:, :])
    x = x_ref[...] + g1_ref[0] * y
    xo_ref[...] = x
    ms = jnp.mean(x * x, axis=-1, keepdims=True)
    fx = (x * lax.rsqrt(ms + EPS) * n2w_ref[...]) * (1.0 + sc2_ref[0]) + sh2_ref[0]
    fx_ref[...] = fx
    f_hi, f_lo = _split2(fx)
    wrh = wrh_ref[...]
    lt_ref[...] = _dot_nt(wrh, f_hi) + _dot_nt(wrh, f_lo) + _dot_nt(wrl_ref[...], f_hi) + br_ref[...]


def _outproj(att, hf, hb, proj, xall, mod3, mnw, n2w, wo, wr_hi, wr_lo, br, *, n_rows, n_lat_rows, seq, nbatch):
    D = xall.shape[1]
    E = wr_hi.shape[0]
    tm = TOK_TM
    n_lat_tiles = n_lat_rows // tm
    per_b = seq // tm

    def mod_idx(which):
        return lambda i: (jnp.where(i < n_lat_tiles, i // per_b, nbatch) * 6 + which, 0, 0)

    row = lambda i: (i, 0)
    const = lambda i: (0, 0)
    return pl.pallas_call(
        _outproj_kernel,
        grid=(n_rows // tm,),
        in_specs=[pl.BlockSpec((tm, ATT_Q_W), row), pl.BlockSpec((tm, ML_V_W), row), pl.BlockSpec((tm, ML_V_W), row),
                  pl.BlockSpec((tm, ML_V_W), lambda i: (i, COL_MO // ML_V_W)),
                  pl.BlockSpec((tm, D), row),
                  pl.BlockSpec((1, 1, D), mod_idx(2)), pl.BlockSpec((1, 1, D), mod_idx(4)),
                  pl.BlockSpec((1, 1, D), mod_idx(3)),
                  pl.BlockSpec((1, ML_V_W), const), pl.BlockSpec((1, D), const),
                  pl.BlockSpec((ATT_Q_W + ML_V_W, D), const),
                  pl.BlockSpec((E, D), const), pl.BlockSpec((E, D), const), pl.BlockSpec((E, 1), const)],
        out_specs=[pl.BlockSpec((tm, D), row), pl.BlockSpec((tm, D), row), pl.BlockSpec((E, tm), lambda i: (0, i))],
        out_shape=[jax.ShapeDtypeStruct((n_rows, D), F32), jax.ShapeDtypeStruct((n_rows, D), F32),
                   jax.ShapeDtypeStruct((E, n_rows), F32)],
        compiler_params=_cparams(("arbitrary",)),
    )(att, hf, hb, proj, xall, mod3, mod3, mod3, mnw.reshape(1, ML_V_W), n2w.reshape(1, D), wo, wr_hi, wr_lo, br)


def _route_kernel(lt_ref, wts_ref, pos_ref, meta_ref, idx_ref, *, chunk, sub, tm):
    E, T = lt_ref.shape
    eio = lax.broadcasted_iota(jnp.int32, (E, chunk), 0)
    ui = lax.broadcasted_iota(jnp.int32, (sub, sub), 0)
    uj = lax.broadcasted_iota(jnp.int32, (sub, sub), 1)
    strict_upper = (ui < uj).astype(BF16)

    carry = jnp.zeros((E, 1), F32)
    for c in range(T // chunk):
        sl = slice(c * chunk, (c + 1) * chunk)
        cur = lt_ref[:, sl]
        member = jnp.zeros((E, chunk), F32)
        vals, hots = [], []
        for k in range(TOP_K):
            mx = jnp.max(cur, axis=0, keepdims=True)
            idx = jnp.min(jnp.where(cur == mx, eio, E), axis=0, keepdims=True)
            hot = eio == idx
            cur = jnp.where(hot, -jnp.inf, cur)
            member = member + hot.astype(F32)
            idx_ref[k:k + 1, sl] = idx
            vals.append(mx)
            hots.append(hot)
        ex = [jnp.exp(v - vals[0]) for v in vals]
        tot = ex[0] + ex[1] + ex[2] + ex[3]
        for k in range(TOP_K):
            wts_ref[k:k + 1, sl] = ex[k] / tot
        ranks = []
        for s in range(chunk // sub):
            mb = member[:, s * sub:(s + 1) * sub]
            ranks.append(_dot(mb.astype(BF16), strict_upper) + carry)
            carry = carry + jnp.sum(mb, axis=1, keepdims=True)
        rank = jnp.concatenate(ranks, axis=1)
        for k in range(TOP_K):
            pos_ref[k:k + 1, sl] = jnp.sum(jnp.where(hots[k], rank, 0.0), axis=0, keepdims=True).astype(jnp.int32)

    ei = lax.broadcasted_iota(jnp.int32, (E, E), 0)
    ej = lax.broadcasted_iota(jnp.int32, (E, E), 1)

    def to_row(col):
        return jnp.sum(jnp.where(ei == ej, col, 0.0), axis=0, keepdims=True)

    def excl_cumsum_col(row):
        return jnp.sum(jnp.where(ej < ei, row, 0.0), axis=1, keepdims=True)

    cnt = carry
    start = excl_cumsum_col(to_row(cnt))
    end = start + cnt

    for c in range(T // chunk):
        sl = slice(c * chunk, (c + 1) * chunk)
        for k in range(TOP_K):
            off = jnp.sum(jnp.where(eio == idx_ref[k:k + 1, sl], start, 0.0), axis=0, keepdims=True)
            pos_ref[k:k + 1, sl] = pos_ref[k:k + 1, sl] + off.astype(jnp.int32)

    nw = meta_ref.shape[1]
    inv = 1.0 / tm
    first_tile = jnp.floor(start * inv)
    last_tile = jnp.floor((end - 1.0) * inv)
    nt = jnp.where(cnt > 0.0, last_tile - first_tile + 1.0, 0.0)
    item_off = excl_cumsum_col(to_row(nt))
    item_end = item_off + nt
    n_items = jnp.sum(nt, axis=0, keepdims=True)
    wi = lax.broadcasted_iota(jnp.int32, (E, nw), 1).astype(F32)
    wv = jnp.minimum(wi, n_items - 1.0)
    e_of = jnp.sum((item_end <= wv).astype(F32), axis=0, keepdims=True)
    sel = lax.broadcasted_iota(jnp.int32, (E, nw), 0).astype(F32) == e_of

    def pick(col):
        return jnp.sum(jnp.where(sel, col, 0.0), axis=0, keepdims=True)

    tile = pick(first_tile) + wv[0:1] - pick(item_off)
    lo = jnp.maximum(pick(start), tile * tm)
    hi = jnp.minimum(pick(end), (tile + 1.0) * tm)
    valid = wi[0:1] < n_items
    hi = jnp.where(valid, hi, lo)
    zeros = jnp.zeros((1, nw), F32)
    meta_ref[...] = jnp.concatenate([tile, e_of, lo, hi, zeros, zeros, zeros, zeros], axis=0).astype(jnp.int32)


def _route(lt, *, tm):
    E, T = lt.shape
    nw = T * TOP_K // tm + E
    nw_pad = -(-nw // LANES) * LANES
    return pl.pallas_call(
        functools.partial(_route_kernel, chunk=2048 if T % 2048 == 0 else 1024, sub=512, tm=tm),
        out_shape=[jax.ShapeDtypeStruct((TOP_K, T), F32), jax.ShapeDtypeStruct((TOP_K, T), jnp.int32),
                   jax.ShapeDtypeStruct((8, nw_pad), jnp.int32)],
        scratch_shapes=[pltpu.VMEM((TOP_K, T), jnp.int32)],
        compiler_params=pltpu.CompilerParams(vmem_limit_bytes=VMEM_LIMIT),
    )(lt)


def _dispatch_kernel(pos_ref, fx_ref, xs_ref, sem, *, n_tok):
    tm = fx_ref.shape[0]
    base = pl.program_id(0) * tm

    def copy(r, k):
        p = pos_ref[k * n_tok + base + r]
        return pltpu.make_async_copy(fx_ref.at[pl.ds(r, 1)], xs_ref.at[pl.ds(p, 1)], sem)

    def start(r, c):
        for k in range(TOP_K):
            copy(r, k).start()
        return c

    def wait(r, c):
        for k in range(TOP_K):
            copy(r, k).wait()
        return c

    lax.fori_loop(0, tm, start, 0)
    lax.fori_loop(0, tm, wait, 0)


def _dispatch(pos_flat, fx):
    T, D = fx.shape
    tm = TOK_TM
    return pl.pallas_call(
        functools.partial(_dispatch_kernel, n_tok=T),
        grid_spec=pltpu.PrefetchScalarGridSpec(
            num_scalar_prefetch=1,
            grid=(T // tm,),
            in_specs=[pl.BlockSpec((tm, D), lambda i, pos: (i, 0))],
            out_specs=pl.BlockSpec(memory_space=pl.ANY),
            scratch_shapes=[pltpu.SemaphoreType.DMA(())]),
        out_shape=jax.ShapeDtypeStruct((T * TOP_K, D), F32),
        compiler_params=_cparams(("arbitrary",)),
    )(pos_flat, fx)


def _moe_kernel(tile_ref, exp_ref, lo_ref, hi_ref, xs_ref, wgu_ref, bgu_ref, wd_ref, bd_ref, ys_ref):
    w = pl.program_id(0)
    tm = xs_ref.shape[0]
    lo, hi = lo_ref[w], hi_ref[w]
    t0 = tile_ref[w] * tm

    @pl.when(hi > lo)
    def _():
        f = wd_ref.shape[1]
        gu = _dot(xs_ref[...].astype(BF16), wgu_ref[0]) + bgu_ref[0]
        gate = jnp.minimum(gu[:, :f], SWIGLU_LIMIT)
        up = jnp.clip(gu[:, f:], -SWIGLU_LIMIT, SWIGLU_LIMIT)
        a = (up + 1.0) * gate * jax.nn.sigmoid(SWIGLU_ALPHA * gate)
        y = _dot(a.astype(BF16), wd_ref[0]) + bd_ref[0]
        rows = t0 + lax.broadcasted_iota(jnp.int32, (tm, 1), 0)
        mine = (rows >= lo) & (rows < hi)

        @pl.when(lo == t0)
        def _():
            ys_ref[...] = jnp.where(mine, y, 0.0)

        @pl.when(lo != t0)
        def _():
            ys_ref[...] = jnp.where(mine, y, ys_ref[...])


def _moe(meta, xs, wgu, bgu, wd, bd):
    P, D = xs.shape
    E, _, F2 = wgu.shape
    F = F2 // 2
    tm = MOE_TM
    nw = P // tm + E
    return pl.pallas_call(
        _moe_kernel,
        grid_spec=pltpu.PrefetchScalarGridSpec(
            num_scalar_prefetch=4,
            grid=(nw,),
            in_specs=[pl.BlockSpec((tm, D), lambda w, t, e, lo, hi: (t[w], 0)),
                      pl.BlockSpec((1, D, F2), lambda w, t, e, lo, hi: (e[w], 0, 0)),
                      pl.BlockSpec((1, 1, F2), lambda w, t, e, lo, hi: (e[w], 0, 0)),
                      pl.BlockSpec((1, F, D), lambda w, t, e, lo, hi: (e[w], 0, 0)),
                      pl.BlockSpec((1, 1, D), lambda w, t, e, lo, hi: (e[w], 0, 0))],
            out_specs=pl.BlockSpec((tm, D), lambda w, t, e, lo, hi: (t[w], 0))),
        out_shape=jax.ShapeDtypeStruct((P, D), F32),
        compiler_params=_cparams(("arbitrary",)),
    )(meta[0], meta[1], meta[2], meta[3], xs, wgu, bgu.reshape(E, 1, F2), wd, bd.reshape(E, 1, D))


def _combine_kernel(pos_ref, ys_ref, x_ref, w_ref, g2_ref, o_ref, buf, sem, *, n_tok):
    tm = x_ref.shape[0]
    base = pl.program_id(0) * tm

    def copy(r, k):
        p = pos_ref[k * n_tok + base + r]
        return pltpu.make_async_copy(ys_ref.at[pl.ds(p, 1)], buf.at[k, pl.ds(r, 1)], sem)

    def start(r, c):
        for k in range(TOP_K):
            copy(r, k).start()
        return c

    def wait(r, c):
        for k in range(TOP_K):
            copy(r, k).wait()
        return c

    lax.fori_loop(0, tm, start, 0)
    lax.fori_loop(0, tm, wait, 0)
    wt = w_ref[...]
    y = wt[:, 0:1] * buf[0]
    for k in range(1, TOP_K):
        y = y + wt[:, k:k + 1] * buf[k]
    o_ref[...] = x_ref[...] + g2_ref[0] * y


def _combine(pos_flat, ys, x, wts_t, mod3, *, n_lat_rows, seq, nbatch):
    T, D = x.shape
    tm = TOK_TM
    n_lat_tiles = n_lat_rows // tm
    per_b = seq // tm
    return pl.pallas_call(
        functools.partial(_combine_kernel, n_tok=T),
        grid_spec=pltpu.PrefetchScalarGridSpec(
            num_scalar_prefetch=1,
            grid=(T // tm,),
            in_specs=[pl.BlockSpec(memory_space=pl.ANY),
                      pl.BlockSpec((tm, D), lambda i, pos: (i, 0)),
                      pl.BlockSpec((tm, TOP_K), lambda i, pos: (i, 0)),
                      pl.BlockSpec((1, 1, D),
                                   lambda i, pos: (jnp.where(i < n_lat_tiles, i // per_b, nbatch) * 6 + 5, 0, 0))],
            out_specs=pl.BlockSpec((tm, D), lambda i, pos: (i, 0)),
            scratch_shapes=[pltpu.VMEM((TOP_K, tm, D), F32), pltpu.SemaphoreType.DMA(())]),
        out_shape=jax.ShapeDtypeStruct((T, D), F32),
        compiler_params=_cparams(("arbitrary",)),
    )(pos_flat, ys, x, wts_t, mod3)


def _permute_w_in(w):
    d = w.shape[0]
    sizes = (ATT_Q_W, ATT_KV_W, ATT_KV_W, ML_QK_W, ML_QK_W, ML_V_W, ML_V_W, N_GATES)
    offs = np.concatenate([[0], np.cumsum(sizes)])
    q, k, v, mq, mk, mv, mo, g = [w[:, offs[i]:offs[i + 1]] for i in range(8)]
    main = jnp.concatenate([q, mv, mo, mq, mk, k, v], axis=1).astype(BF16)
    g_pad = jnp.concatenate([g, jnp.zeros((d, LANES - N_GATES), w.dtype)], axis=1).astype(BF16)
    return main, g_pad, g.T.astype(BF16)


def kernel(x, c, ctx, c_ctx, w_ada, b_ada, norm1_w, norm2_w, w_in, b_gates, q_norm_w, k_norm_w, attn_sink,
           mlstm_norm_w, w_out, w_router, b_router, w_gate_up, b_gate_up, w_down, b_down):
    B, S, D = x.shape
    C = ctx.shape[1]
    depth = w_ada.shape[0]
    E = w_router.shape[-1]
    n_lat_rows = B * S
    R = n_lat_rows + B * C
    assert B < 16 and S % 1024 == 0 and C % 256 == 0 and (B * C) % 1024 == 0 and D % 128 == 0

    xall = jnp.concatenate([x.reshape(n_lat_rows, D), ctx.reshape(B * C, D)], axis=0)
    cv = jnp.zeros((16, D), F32).at[:B].set(c).at[B].set(c_ctx)
    mod = _ada_mod(cv, w_ada, b_ada)

    rope_tm = 512
    tabs = _rope_tables(S, rope_tm)
    reps = ATT_KV_W // ATT_HEAD_DIM
    gi = np.arange(ATT_KV_W) // ATT_HEAD_DIM
    bd = jnp.asarray(gi[:, None] == gi[None, :], BF16)

    for l in range(depth):
        last = l == depth - 1
        mod3 = mod[l].reshape(16 * 6, 1, D)
        w_main, w_g, w_gt = _permute_w_in(w_in[l])
        proj, g, gt = _inproj(xall, mod3, norm1_w[l], w_main, w_g, w_gt, n_lat_rows=n_lat_rows, seq=S, nbatch=B)
        qk = _qkrope(proj, tabs, jnp.tile(q_norm_w[l], reps).reshape(1, -1), jnp.tile(k_norm_w[l], reps).reshape(1, -1),
                     bd, n_lat_rows=n_lat_rows, seq=S)
        att = _attention(qk, proj, attn_sink[l], nbatch=B, seq=S, ctx_len=C, with_ctx_queries=not last)
        hf, hb = _mlstm(proj, g, gt, b_gates[l], nbatch=B, seq=S, ctx_len=C)

        T = n_lat_rows if last else R
        wr_t = w_router[l].T
        wr_hi = wr_t.astype(BF16)
        wr_lo = (wr_t - wr_hi.astype(F32)).astype(BF16)
        xmid, fx, lt = _outproj(att, hf, hb, proj, xall, mod3, mlstm_norm_w[l], norm2_w[l], w_out[l].astype(BF16),
                                wr_hi, wr_lo, b_router[l].reshape(E, 1),
                                n_rows=T, n_lat_rows=n_lat_rows, seq=S, nbatch=B)
        wts, pos, meta = _route(lt, tm=MOE_TM)
        pos_flat = pos.reshape(-1)
        xs = _dispatch(pos_flat, fx)
        ys = _moe(meta, xs, w_gate_up[l].astype(BF16), b_gate_up[l], w_down[l].astype(BF16), b_down[l])
        xall = _combine(pos_flat, ys, xmid, wts.T, mod3, n_lat_rows=n_lat_rows, seq=S, nbatch=B)

    return xall[:n_lat_rows].reshape(B, S, D)
```

```python
import functools

import numpy as np
import jax
import jax.numpy as jnp
from jax import lax
from jax.experimental import pallas as pl
from jax.experimental.pallas import tpu as pltpu

F32 = jnp.float32
BF16 = jnp.bfloat16

GRID_W = 64
EPS = 1e-6
NEG_INF = -1e30
ATT_HEADS = 16
ATT_KV_HEADS = 4
ATT_HEAD_DIM = 64
ATT_GROUP = ATT_HEADS // ATT_KV_HEADS
WINDOW = 128
ATT_BLOCK = 128
ROPE_THETA = 10000.0
ML_HEADS = 4
ML_DQK = 128
ML_DV = 256
ML_CHUNK = 128
GATE_SOFTCAP = 15.0
TOP_K = 4
SWIGLU_LIMIT = 7.0
SWIGLU_ALPHA = 1.702

ATT_Q_W = ATT_HEADS * ATT_HEAD_DIM
ATT_KV_W = ATT_KV_HEADS * ATT_HEAD_DIM
ML_QK_W = ML_HEADS * ML_DQK
ML_V_W = ML_HEADS * ML_DV
N_GATES = 4 * ML_HEADS

COL_Q = 0
COL_MV = COL_Q + ATT_Q_W
COL_MO = COL_MV + ML_V_W
COL_MQ = COL_MO + ML_V_W
COL_MK = COL_MQ + ML_QK_W
COL_K = COL_MK + ML_QK_W
COL_V = COL_K + ATT_KV_W
PROJ_W = COL_V + ATT_KV_W
LOG2E = 1.4426950408889634
Q_SCALE = ATT_HEAD_DIM ** -0.5 * LOG2E

LANES = 128
VMEM_LIMIT = 56 * 1024 * 1024

MOE_TM = 256
TOK_TM = 256
GATHER_TM = 512


def _dot(a, b):
    return jnp.dot(a, b, preferred_element_type=F32)


def _dot_nt(a, b):
    return lax.dot_general(a, b, (((1,), (1,)), ((), ())), preferred_element_type=F32)


def _split2(a):
    hi = a.astype(BF16)
    lo = (a - hi.astype(F32)).astype(BF16)
    return hi, lo


def _cparams(sem):
    return pltpu.CompilerParams(dimension_semantics=sem, vmem_limit_bytes=VMEM_LIMIT)


def _ada_kernel(cv_ref, w_ref, b_ref, o_ref):
    cv = cv_ref[...]
    s = cv * jax.nn.sigmoid(cv)
    o_ref[0] = _dot(s.astype(BF16), w_ref[0].astype(BF16)) + b_ref[0]


def _ada_mod(cv, w_ada, b_ada):
    L, D, N = w_ada.shape
    tn = 1024
    return pl.pallas_call(
        _ada_kernel,
        grid=(L, N // tn),
        in_specs=[pl.BlockSpec((16, D), lambda l, j: (0, 0)),
                  pl.BlockSpec((1, D, tn), lambda l, j: (l, 0, j)),
                  pl.BlockSpec((1, 1, tn), lambda l, j: (l, 0, j))],
        out_specs=pl.BlockSpec((1, 16, tn), lambda l, j: (l, 0, j)),
        out_shape=jax.ShapeDtypeStruct((L, 16, N), F32),
        compiler_params=_cparams(("arbitrary", "arbitrary")),
    )(cv, w_ada, b_ada.reshape(L, 1, N))


def _inproj_kernel(x_ref, sc_ref, sh_ref, nw_ref, w_ref, wg_ref, wgt_ref, p_ref, g_ref, gt_ref, hn_ref):
    @pl.when(pl.program_id(1) == 0)
    def _():
        x = x_ref[...]
        ms = jnp.mean(x * x, axis=-1, keepdims=True)
        y = x * lax.rsqrt(ms + EPS) * nw_ref[...]
        hb = (y * (1.0 + sc_ref[0]) + sh_ref[0]).astype(BF16)
        hn_ref[...] = hb
        g_ref[...] = _dot(hb, wg_ref[0])
        gt_ref[...] = _dot_nt(wgt_ref[0], hb)

    p_ref[...] = _dot(hn_ref[...], w_ref[0]).astype(BF16)


def _inproj(xall, mod3, norm_w, w_main, w_g, w_gt, layer, *, n_lat_rows, seq, nbatch):
    R, D = xall.shape
    tm, tn = 1024, 1536
    n_lat_tiles = n_lat_rows // tm
    per_b = seq // tm

    def mod_idx(which):
        return lambda i, j: (jnp.where(i < n_lat_tiles, i // per_b, nbatch) * 6 + which, 0, 0)

    return pl.pallas_call(
        _inproj_kernel,
        grid=(R // tm, PROJ_W // tn),
        in_specs=[pl.BlockSpec((tm, D), lambda i, j: (i, 0)),
                  pl.BlockSpec((1, 1, D), mod_idx(1)),
                  pl.BlockSpec((1, 1, D), mod_idx(0)),
                  pl.BlockSpec((1, D), lambda i, j: (0, 0)),
                  pl.BlockSpec((1, D, tn), lambda i, j: (layer, 0, j)),
                  pl.BlockSpec((1, D, LANES), lambda i, j: (layer, 0, 0)),
                  pl.BlockSpec((1, N_GATES, D), lambda i, j: (layer, 0, 0))],
        out_specs=[pl.BlockSpec((tm, tn), lambda i, j: (i, j)),
                   pl.BlockSpec((tm, LANES), lambda i, j: (i, 0)),
                   pl.BlockSpec((N_GATES, tm), lambda i, j: (0, i))],
        out_shape=[jax.ShapeDtypeStruct((R, PROJ_W), BF16),
                   jax.ShapeDtypeStruct((R, LANES), F32),
                   jax.ShapeDtypeStruct((N_GATES, R), F32)],
        scratch_shapes=[pltpu.VMEM((tm, D), BF16)],
        compiler_params=_cparams(("arbitrary", "arbitrary")),
    )(xall, mod3, mod3, norm_w.reshape(1, D), w_main, w_g, w_gt)


def _norm_rope(x, w_row, cos, sa, sb, bd, scale):
    x = x.astype(F32)
    hi, lo = _split2(x * x)
    ss = _dot(hi, bd) + _dot(lo, bd)
    y = x * lax.rsqrt(ss * (1.0 / ATT_HEAD_DIM) + EPS) * w_row
    w = y.shape[-1]
    r = y * cos + pltpu.roll(y, 16, 1) * sa + pltpu.roll(y, w - 16, 1) * sb
    if scale != 1.0:
        r = r * scale
    return r


def _qkrope_kernel(q_ref, k_ref, cos_ref, sa_ref, sb_ref, qw_ref, kw_ref, bd_ref, q_out, kt_out):
    cos, sa, sb, bd = cos_ref[...], sa_ref[...], sb_ref[...], bd_ref[...]
    cw = cos.shape[-1]
    for c in range(ATT_Q_W // cw):
        q_out[:, c * cw:(c + 1) * cw] = _norm_rope(q_ref[:, c * cw:(c + 1) * cw], qw_ref[...], cos, sa, sb, bd,
                                                   Q_SCALE).astype(BF16)
    kt_out[...] = _norm_rope(k_ref[...], kw_ref[...], cos, sa, sb, bd, 1.0).T.astype(BF16)


def _qkrope(proj, tabs, qw, kw, bd, *, n_lat_rows, seq):
    R = proj.shape[0]
    tm = 512
    n_lat_tiles = n_lat_rows // tm
    per_b = seq // tm
    cw = ATT_KV_W
    tab_spec = pl.BlockSpec((tm, cw), lambda i: (jnp.where(i < n_lat_tiles, i % per_b, per_b), 0))
    const = lambda i: (0, 0)
    return pl.pallas_call(
        _qkrope_kernel,
        grid=(R // tm,),
        in_specs=[pl.BlockSpec((tm, ATT_Q_W), lambda i: (i, 0)),
                  pl.BlockSpec((tm, ATT_KV_W), lambda i: (i, COL_K // ATT_KV_W)),
                  tab_spec, tab_spec, tab_spec,
                  pl.BlockSpec((1, cw), const), pl.BlockSpec((1, cw), const), pl.BlockSpec((cw, cw), const)],
        out_specs=[pl.BlockSpec((tm, ATT_Q_W), lambda i: (i, 0)),
                   pl.BlockSpec((ATT_KV_W, tm), lambda i: (0, i))],
        out_shape=[jax.ShapeDtypeStruct((R, ATT_Q_W), BF16), jax.ShapeDtypeStruct((ATT_KV_W, R), BF16)],
        compiler_params=_cparams(("arbitrary",)),
    )(proj, proj, tabs[0], tabs[1], tabs[2], qw, kw, bd)


def _rope_tables(seq, tm):
    rows = seq // GRID_W
    row = jnp.repeat(jnp.arange(rows, dtype=F32), GRID_W)
    col = jnp.tile(jnp.arange(GRID_W, dtype=F32), rows)
    half = ATT_HEAD_DIM // 2
    inv_freq = ROPE_THETA ** (-jnp.arange(0, half, 2, dtype=F32) / half)

    def ang(p):
        a = p[:, None] * inv_freq[None, :]
        return jnp.concatenate([a, a], axis=-1)

    a = jnp.concatenate([ang(row), ang(col)], axis=-1)
    cos, sin = jnp.cos(a), jnp.sin(a)
    second = (jnp.arange(ATT_HEAD_DIM) % half) >= (half // 2)
    sa = jnp.where(second[None, :], sin, 0.0)
    sb = jnp.where(second[None, :], 0.0, -sin)
    reps = ATT_KV_W // ATT_HEAD_DIM

    def fin(t, fill):
        t = jnp.tile(t, (1, reps))
        return jnp.concatenate([t, jnp.full((tm, t.shape[1]), fill, F32)], axis=0)

    return fin(cos, 1.0), fin(sa, 0.0), fin(sb, 0.0)


def _attn_kernel(sink_ref, q_ref, kp_ref, kc_ref, kn_ref, vp_ref, vc_ref, vn_ref, kx_ref, vx_ref, o_ref, *, n_lat):
    n = pl.program_id(1)
    blk, hd = ATT_BLOCK, ATT_HEAD_DIM
    n_loc = 3 * blk
    kt = jnp.concatenate([kp_ref[...], kc_ref[...], kn_ref[...], kx_ref[...]], axis=1)
    v = jnp.concatenate([vp_ref[...], vc_ref[...], vn_ref[...], vx_ref[...]], axis=0)
    nk = kt.shape[1]
    row = lax.broadcasted_iota(jnp.int32, (2 * blk, nk), 0)
    r = jnp.where(row >= blk, row - blk, row)
    w = lax.broadcasted_iota(jnp.int32, (2 * blk, nk), 1)
    rel = w - r
    jabs = n * blk - WINDOW + w
    ok = (w >= n_loc) | ((rel >= 0) & (rel <= 2 * WINDOW) & (jabs >= 0) & (jabs < n_lat * blk) & (n < n_lat))
    lo_half = lax.broadcasted_iota(jnp.int32, (1, 2 * hd), 1) < hd
    is_a = lax.broadcasted_iota(jnp.int32, (2 * blk, 1), 0) < blk
    zero = jnp.zeros((), BF16)
    one = jnp.ones((), BF16)
    for h in range(ATT_KV_HEADS):
        kth = kt[h * hd:(h + 1) * hd, :]
        rhs_s = jnp.concatenate([kth, kth], axis=0)
        vt = v[:, (h // 2) * 2 * hd:(h // 2 + 1) * 2 * hd]
        v_lo = vt if h % 2 == 0 else jnp.concatenate([vt[:, hd:], vt[:, :hd]], axis=1)
        rhs_v = jnp.where(lo_half, v_lo, one)
        for pr in range(ATT_GROUP // 2):
            ha = h * ATT_GROUP + 2 * pr
            qp = q_ref[:, ha * hd:(ha + 2) * hd]
            lhs = jnp.concatenate([jnp.where(lo_half, qp, zero), jnp.where(lo_half, zero, qp)], axis=0)
            s = jnp.where(ok, _dot(lhs, rhs_s), NEG_INF)
            sink = jnp.where(is_a, sink_ref[ha], sink_ref[ha + 1]) * LOG2E
            m = jnp.maximum(jnp.max(s, axis=-1, keepdims=True), sink)
            res = _dot(jnp.exp2(s - m).astype(BF16), rhs_v)
            e_sink = jnp.exp2(sink - m)
            ra, rb = res[:blk], res[blk:]
            num = jnp.where(lo_half, ra, pltpu.roll(rb, hd, 1))
            den = (jnp.where(lo_half, pltpu.roll(ra, hd, 1), rb)
                   + jnp.where(lo_half, e_sink[:blk], e_sink[blk:]))
            o_ref[:, ha * hd:(ha + 2) * hd] = (num / den).astype(BF16)


def _attention(q, kt, proj, sink, *, nbatch, seq, ctx_len, with_ctx_queries):
    blk = ATT_BLOCK
    n_lat = seq // blk
    n_ctx = ctx_len // blk
    nb = n_lat + (n_ctx if with_ctx_queries else 0)
    lat_blocks = nbatch * n_lat
    out_rows = nbatch * nb * blk
    vcol = COL_V // ATT_KV_W

    def qrow(b, n):
        return jnp.where(n < n_lat, b * n_lat + n, lat_blocks + b * n_ctx + (n - n_lat))

    def win(off):
        return lambda b, n: b * n_lat + jnp.clip(n + off, 0, n_lat - 1)

    def spec_k(off):
        f = win(off)
        return pl.BlockSpec((ATT_KV_W, blk), lambda b, n: (0, f(b, n)))

    def spec_v(off):
        f = win(off)
        return pl.BlockSpec((blk, ATT_KV_W), lambda b, n: (f(b, n), vcol))

    ctx_row = lambda b: (nbatch * seq) // ctx_len + b
    return pl.pallas_call(
        functools.partial(_attn_kernel, n_lat=n_lat),
        grid=(nbatch, nb),
        in_specs=[pl.BlockSpec(memory_space=pltpu.SMEM),
                  pl.BlockSpec((blk, ATT_Q_W), lambda b, n: (qrow(b, n), 0)),
                  spec_k(-1), spec_k(0), spec_k(1), spec_v(-1), spec_v(0), spec_v(1),
                  pl.BlockSpec((ATT_KV_W, ctx_len), lambda b, n: (0, ctx_row(b))),
                  pl.BlockSpec((ctx_len, ATT_KV_W), lambda b, n: (ctx_row(b), vcol))],
        out_specs=pl.BlockSpec((blk, ATT_Q_W), lambda b, n: (qrow(b, n), 0)),
        out_shape=jax.ShapeDtypeStruct((out_rows, ATT_Q_W), BF16),
        compiler_params=_cparams(("arbitrary", "arbitrary")),
    )(sink, q, kt, kt, kt, proj, proj, proj, kt, proj)


def _softcap(g):
    return GATE_SOFTCAP * jnp.tanh(g * (1.0 / GATE_SOFTCAP))


def _log_sigmoid(x):
    return jnp.minimum(x, 0.0) - jnp.log(1.0 + jnp.exp(-jnp.abs(x)))


def _mlstm_kernel(bg_ref, bgt_ref, gf_ref, gtf_ref, gb_ref, gtb_ref, qf_ref, kf_ref, vf_ref, qb_ref, kb_ref, vb_ref,
                  hf_ref, hb_ref, c_ref, n_ref, m_ref):
    L = ML_CHUNK

    @pl.when(pl.program_id(1) == 0)
    def _():
        c_ref[...] = jnp.zeros_like(c_ref)
        n_ref[...] = jnp.zeros_like(n_ref)
        m_ref[...] = jnp.zeros_like(m_ref)

    ri = lax.broadcasted_iota(jnp.int32, (L, L), 0)
    ci = lax.broadcasted_iota(jnp.int32, (L, L), 1)
    lower = ci <= ri
    upper = ci >= ri
    lower_b = lower.astype(BF16)
    upper_b = upper.astype(BF16)
    scale = ML_DQK ** -0.5

    dirs = ((gf_ref, gtf_ref, qf_ref, kf_ref, vf_ref, hf_ref, lower, lower_b, upper_b, L - 1),
            (gb_ref, gtb_ref, qb_ref, kb_ref, vb_ref, hb_ref, upper, upper_b, lower_b, 0))
    for d, (g_ref, gt_ref, q_ref, k_ref, v_ref, h_ref, tri, tri_b, trit_b, last) in enumerate(dirs):
        g = _softcap(g_ref[...] + bg_ref[...])
        gt = _softcap(gt_ref[...] + bgt_ref[...])
        lf_hi, lf_lo = _split2(_log_sigmoid(g))
        lft_hi, lft_lo = _split2(_log_sigmoid(gt))
        bcol_all = _dot(tri_b, lf_hi) + _dot(tri_b, lf_lo)
        brow_all = _dot(lft_hi, trit_b) + _dot(lft_lo, trit_b)
        for h in range(ML_HEADS):
            r = d * ML_HEADS + h
            ic = d * 2 * ML_HEADS + h
            fc = ic + ML_HEADS
            i_col, b_col = g[:, ic:ic + 1], bcol_all[:, fc:fc + 1]
            i_row, b_row = gt[ic:ic + 1, :], brow_all[fc:fc + 1, :]
            total = b_col[last:last + 1, :]
            m_prev = m_ref[r:r + 1, 0:1]
            inter = b_col + m_prev
            dmat = jnp.where(tri, b_col - b_row + i_row, NEG_INF)
            m_t = jnp.maximum(inter, jnp.max(dmat, axis=-1, keepdims=True))
            q = q_ref[:, h * ML_DQK:(h + 1) * ML_DQK]
            k_s = k_ref[:, h * ML_DQK:(h + 1) * ML_DQK].astype(F32) * scale
            v = v_ref[:, h * ML_DV:(h + 1) * ML_DV]
            w_intra = _dot_nt(q, k_s.astype(BF16)) * jnp.exp(dmat - m_t)
            w_inter = jnp.exp(inter - m_t)
            c_t = c_ref[r]
            n_row = n_ref[r:r + 1, :]
            num = _dot(w_intra.astype(BF16), v) + w_inter * _dot(q, c_t.astype(BF16))
            qn = jnp.sum(q.astype(F32) * n_row, axis=-1, keepdims=True)
            den = jnp.sum(w_intra, axis=-1, keepdims=True) + w_inter * qn
            h_out = num / jnp.maximum(jnp.abs(den), jnp.exp(-m_t))
            h_ref[:, h * ML_DV:(h + 1) * ML_DV] = h_out.astype(BF16)
            g_row = total - b_row + i_row
            g_col = total - b_col + i_col
            m_new = jnp.maximum(total + m_prev, jnp.max(g_row, axis=-1, keepdims=True))
            ws_col = jnp.exp(g_col - m_new)
            w_c = jnp.exp(total + m_prev - m_new)
            c_ref[r] = w_c * c_t + _dot(k_s.T.astype(BF16), (ws_col * v.astype(F32)).astype(BF16))
            n_ref[r:r + 1, :] = w_c * n_row + jnp.sum(ws_col * k_s, axis=0, keepdims=True)
            m_ref[r:r + 1, :] = jnp.broadcast_to(m_new, (1, LANES))


def _mlstm(proj, g, gt, b_gates, *, nbatch, seq, ctx_len):
    L = ML_CHUNK
    R = proj.shape[0]
    n_lat = seq // L
    n_ctx = ctx_len // L
    steps = n_ctx + n_lat
    lat_blocks = nbatch * n_lat

    def fwd(b, t):
        return jnp.where(t < n_ctx, lat_blocks + b * n_ctx + t, b * n_lat + t - n_ctx)

    def bwd(b, t):
        return jnp.where(t < n_ctx, lat_blocks + b * n_ctx + (n_ctx - 1 - t), b * n_lat + (steps - 1 - t))

    def rows(f, width, colblk):
        return pl.BlockSpec((L, width), lambda b, t: (f(b, t), colblk))

    def gt_spec(f):
        return pl.BlockSpec((N_GATES, L), lambda b, t: (0, f(b, t)))

    bg = jnp.zeros((1, LANES), F32).at[0, :N_GATES].set(b_gates)
    bgt = b_gates.reshape(N_GATES, 1)
    const = lambda b, t: (0, 0)
    qc, kc, vc = COL_MQ // ML_QK_W, COL_MK // ML_QK_W, COL_MV // ML_V_W
    return pl.pallas_call(
        _mlstm_kernel,
        grid=(nbatch, steps),
        in_specs=[pl.BlockSpec((1, LANES), const), pl.BlockSpec((N_GATES, 1), const),
                  rows(fwd, LANES, 0), gt_spec(fwd), rows(bwd, LANES, 0), gt_spec(bwd),
                  rows(fwd, ML_QK_W, qc), rows(fwd, ML_QK_W, kc), rows(fwd, ML_V_W, vc),
                  rows(bwd, ML_QK_W, qc), rows(bwd, ML_QK_W, kc), rows(bwd, ML_V_W, vc)],
        out_specs=[rows(fwd, ML_V_W, 0), rows(bwd, ML_V_W, 0)],
        out_shape=[jax.ShapeDtypeStruct((R, ML_V_W), BF16), jax.ShapeDtypeStruct((R, ML_V_W), BF16)],
        scratch_shapes=[pltpu.VMEM((2 * ML_HEADS, ML_DQK, ML_DV), F32),
                        pltpu.VMEM((2 * ML_HEADS, ML_DQK), F32),
                        pltpu.VMEM((2 * ML_HEADS, LANES), F32)],
        compiler_params=_cparams(("arbitrary", "arbitrary")),
    )(bg, bgt, g, gt, g, gt, proj, proj, proj, proj, proj, proj)


def _outproj_kernel(att_ref, hf_ref, hb_ref, mo_ref, x_ref, g1_ref, sc2_ref, sh2_ref, mnw_ref, n2w_ref,
                    wo_ref, wrh_ref, wrl_ref, br_ref, xo_ref, fx_ref, lt_ref):
    hsum = hf_ref[...].astype(F32) + hb_ref[...].astype(F32)
    parts = []
    for h in range(ML_HEADS):
        hh = hsum[:, h * ML_DV:(h + 1) * ML_DV]
        ms = jnp.mean(hh * hh, axis=-1, keepdims=True)
        parts.append(hh * lax.rsqrt(ms + EPS))
    ml = jnp.concatenate(parts, axis=1) * mnw_ref[...] * jax.nn.sigmoid(mo_ref[...].astype(F32))
    y = _dot(att_ref[...], wo_ref[0, :ATT_Q_W, :]) + _dot(ml.astype(BF16), wo_ref[0, ATT_Q_W:, :])
    x = x_ref[...] + g1_ref[0] * y
    xo_ref[...] = x
    ms = jnp.mean(x * x, axis=-1, keepdims=True)
    fx = (x * lax.rsqrt(ms + EPS) * n2w_ref[...]) * (1.0 + sc2_ref[0]) + sh2_ref[0]
    fx_ref[...] = fx
    f_hi, f_lo = _split2(fx)
    wrh = wrh_ref[...]
    lt_ref[...] = _dot_nt(wrh, f_hi) + _dot_nt(wrh, f_lo) + _dot_nt(wrl_ref[...], f_hi) + br_ref[...]


def _outproj(att, hf, hb, proj, xall, mod3, mnw, n2w, wo, wr_hi, wr_lo, br, layer, *, n_rows, n_lat_rows, seq,
             nbatch):
    D = xall.shape[1]
    E = wr_hi.shape[0]
    tm = TOK_TM
    n_lat_tiles = n_lat_rows // tm
    per_b = seq // tm

    def mod_idx(which):
        return lambda i: (jnp.where(i < n_lat_tiles, i // per_b, nbatch) * 6 + which, 0, 0)

    row = lambda i: (i, 0)
    const = lambda i: (0, 0)
    return pl.pallas_call(
        _outproj_kernel,
        grid=(n_rows // tm,),
        in_specs=[pl.BlockSpec((tm, ATT_Q_W), row), pl.BlockSpec((tm, ML_V_W), row), pl.BlockSpec((tm, ML_V_W), row),
                  pl.BlockSpec((tm, ML_V_W), lambda i: (i, COL_MO // ML_V_W)),
                  pl.BlockSpec((tm, D), row),
                  pl.BlockSpec((1, 1, D), mod_idx(2)), pl.BlockSpec((1, 1, D), mod_idx(4)),
                  pl.BlockSpec((1, 1, D), mod_idx(3)),
                  pl.BlockSpec((1, ML_V_W), const), pl.BlockSpec((1, D), const),
                  pl.BlockSpec((1, ATT_Q_W + ML_V_W, D), lambda i: (layer, 0, 0)),
                  pl.BlockSpec((E, D), const), pl.BlockSpec((E, D), const), pl.BlockSpec((E, 1), const)],
        out_specs=[pl.BlockSpec((tm, D), row), pl.BlockSpec((tm, D), row), pl.BlockSpec((E, tm), lambda i: (0, i))],
        out_shape=[jax.ShapeDtypeStruct((n_rows, D), F32), jax.ShapeDtypeStruct((n_rows, D), F32),
                   jax.ShapeDtypeStruct((E, n_rows), F32)],
        compiler_params=_cparams(("arbitrary",)),
    )(att, hf, hb, proj, xall, mod3, mod3, mod3, mnw.reshape(1, ML_V_W), n2w.reshape(1, D), wo, wr_hi, wr_lo, br)


def _route_kernel(lt_ref, wts_ref, pos_ref, meta_ref, idx_ref, *, chunk, sub, tm):
    E, T = lt_ref.shape
    eio = lax.broadcasted_iota(jnp.int32, (E, chunk), 0)
    ui = lax.broadcasted_iota(jnp.int32, (sub, sub), 0)
    uj = lax.broadcasted_iota(jnp.int32, (sub, sub), 1)
    strict_upper = (ui < uj).astype(BF16)

    carry = jnp.zeros((E, 1), F32)
    for c in range(T // chunk):
        sl = slice(c * chunk, (c + 1) * chunk)
        cur = lt_ref[:, sl]
        member = jnp.zeros((E, chunk), F32)
        vals, hots = [], []
        for k in range(TOP_K):
            mx = jnp.max(cur, axis=0, keepdims=True)
            idx = jnp.min(jnp.where(cur == mx, eio, E), axis=0, keepdims=True)
            hot = eio == idx
            cur = jnp.where(hot, -jnp.inf, cur)
            member = member + hot.astype(F32)
            idx_ref[k:k + 1, sl] = idx
            vals.append(mx)
            hots.append(hot)
        ex = [jnp.exp(v - vals[0]) for v in vals]
        tot = ex[0] + ex[1] + ex[2] + ex[3]
        for k in range(TOP_K):
            wts_ref[k:k + 1, sl] = ex[k] / tot
        ranks = []
        for s in range(chunk // sub):
            mb = member[:, s * sub:(s + 1) * sub]
            ranks.append(_dot(mb.astype(BF16), strict_upper) + carry)
            carry = carry + jnp.sum(mb, axis=1, keepdims=True)
        rank = jnp.concatenate(ranks, axis=1)
        for k in range(TOP_K):
            pos_ref[k:k + 1, sl] = jnp.sum(jnp.where(hots[k], rank, 0.0), axis=0, keepdims=True).astype(jnp.int32)

    ei = lax.broadcasted_iota(jnp.int32, (E, E), 0)
    ej = lax.broadcasted_iota(jnp.int32, (E, E), 1)

    def to_row(col):
        return jnp.sum(jnp.where(ei == ej, col, 0.0), axis=0, keepdims=True)

    def excl_cumsum_col(row):
        return jnp.sum(jnp.where(ej < ei, row, 0.0), axis=1, keepdims=True)

    cnt = carry
    start = excl_cumsum_col(to_row(cnt))
    end = start + cnt

    for c in range(T // chunk):
        sl = slice(c * chunk, (c + 1) * chunk)
        for k in range(TOP_K):
            off = jnp.sum(jnp.where(eio == idx_ref[k:k + 1, sl], start, 0.0), axis=0, keepdims=True)
            pos_ref[k:k + 1, sl] = pos_ref[k:k + 1, sl] + off.astype(jnp.int32)

    nw = meta_ref.shape[1]
    inv = 1.0 / tm
    first_tile = jnp.floor(start * inv)
    last_tile = jnp.floor((end - 1.0) * inv)
    nt = jnp.where(cnt > 0.0, last_tile - first_tile + 1.0, 0.0)
    item_off = excl_cumsum_col(to_row(nt))
    item_end = item_off + nt
    n_items = jnp.sum(nt, axis=0, keepdims=True)
    wi = lax.broadcasted_iota(jnp.int32, (E, nw), 1).astype(F32)
    wv = jnp.minimum(wi, n_items - 1.0)
    e_of = jnp.sum((item_end <= wv).astype(F32), axis=0, keepdims=True)
    sel = lax.broadcasted_iota(jnp.int32, (E, nw), 0).astype(F32) == e_of

    def pick(col):
        return jnp.sum(jnp.where(sel, col, 0.0), axis=0, keepdims=True)

    tile = pick(first_tile) + wv[0:1] - pick(item_off)
    lo = jnp.maximum(pick(start), tile * tm)
    hi = jnp.minimum(pick(end), (tile + 1.0) * tm)
    valid = wi[0:1] < n_items
    hi = jnp.where(valid, hi, lo)
    zeros = jnp.zeros((1, nw), F32)
    meta_ref[...] = jnp.concatenate([tile, e_of, lo, hi, zeros, zeros, zeros, zeros], axis=0).astype(jnp.int32)


def _route(lt, *, tm):
    E, T = lt.shape
    nw = T * TOP_K // tm + E
    nw_pad = -(-nw // LANES) * LANES
    return pl.pallas_call(
        functools.partial(_route_kernel, chunk=2048 if T % 2048 == 0 else 1024, sub=512, tm=tm),
        out_shape=[jax.ShapeDtypeStruct((TOP_K, T), F32), jax.ShapeDtypeStruct((TOP_K, T), jnp.int32),
                   jax.ShapeDtypeStruct((8, nw_pad), jnp.int32)],
        scratch_shapes=[pltpu.VMEM((TOP_K, T), jnp.int32)],
        compiler_params=pltpu.CompilerParams(vmem_limit_bytes=VMEM_LIMIT),
    )(lt)


def _dispatch_kernel(pos_ref, fx_ref, xs_ref, sem, *, n_tok):
    tm = fx_ref.shape[0]
    base = pl.program_id(0) * tm

    def copy(r, k):
        p = pos_ref[k * n_tok + base + r]
        return pltpu.make_async_copy(fx_ref.at[pl.ds(r, 1)], xs_ref.at[pl.ds(p, 1)], sem)

    def start(r, c):
        for k in range(TOP_K):
            copy(r, k).start()
        return c

    lax.fori_loop(0, tm, start, 0)
    for k in range(TOP_K):
        pltpu.make_async_copy(fx_ref, xs_ref.at[pl.ds(0, tm)], sem).wait()


def _dispatch(pos_flat, fx):
    T, D = fx.shape
    tm = GATHER_TM
    return pl.pallas_call(
        functools.partial(_dispatch_kernel, n_tok=T),
        grid_spec=pltpu.PrefetchScalarGridSpec(
            num_scalar_prefetch=1,
            grid=(T // tm,),
            in_specs=[pl.BlockSpec((tm, D), lambda i, pos: (i, 0))],
            out_specs=pl.BlockSpec(memory_space=pl.ANY),
            scratch_shapes=[pltpu.SemaphoreType.DMA(())]),
        out_shape=jax.ShapeDtypeStruct((T * TOP_K, D), F32),
        compiler_params=_cparams(("arbitrary",)),
    )(pos_flat, fx)


def _expert_linear_kernel(tile_ref, exp_ref, lo_ref, hi_ref, x_ref, w_ref, b_ref, o_ref, wbf_ref, *, swiglu):
    w = pl.program_id(0)
    tm = x_ref.shape[0]
    lo, hi = lo_ref[w], hi_ref[w]
    t0 = tile_ref[w] * tm

    @pl.when(hi > lo)
    def _():
        @pl.when((w == 0) | (exp_ref[w] != exp_ref[jnp.maximum(w - 1, 0)]))
        def _():
            rows_per = min(256, wbf_ref.shape[0])

            def cast(i, c):
                sl = pl.ds(pl.multiple_of(i * rows_per, rows_per), rows_per)
                wbf_ref[sl, :] = w_ref[0, 0, sl, :].astype(BF16)
                return c

            lax.fori_loop(0, wbf_ref.shape[0] // rows_per, cast, 0)

        y = _dot(x_ref[...].astype(BF16), wbf_ref[...]) + b_ref[0, 0]
        if swiglu:
            f = y.shape[1] // 2
            gate = jnp.minimum(y[:, :f], SWIGLU_LIMIT)
            up = jnp.clip(y[:, f:], -SWIGLU_LIMIT, SWIGLU_LIMIT)
            y = (up + 1.0) * gate * jax.nn.sigmoid(SWIGLU_ALPHA * gate)
        y = y.astype(o_ref.dtype)
        rows = t0 + lax.broadcasted_iota(jnp.int32, (tm, 1), 0)
        mine = (rows >= lo) & (rows < hi)

        @pl.when(lo == t0)
        def _():
            o_ref[...] = jnp.where(mine, y, jnp.zeros_like(y))

        @pl.when(lo != t0)
        def _():
            o_ref[...] = jnp.where(mine, y, o_ref[...])


def _expert_linear(meta, x, w_all, b_all, layer, *, swiglu, out_dtype):
    P, K = x.shape
    _, E, _, N = w_all.shape
    n_out = N // 2 if swiglu else N
    tm = MOE_TM
    nw = P // tm + E
    return pl.pallas_call(
        functools.partial(_expert_linear_kernel, swiglu=swiglu),
        grid_spec=pltpu.PrefetchScalarGridSpec(
            num_scalar_prefetch=4,
            grid=(nw,),
            in_specs=[pl.BlockSpec((tm, K), lambda w, t, e, lo, hi: (t[w], 0)),
                      pl.BlockSpec((1, 1, K, N), lambda w, t, e, lo, hi: (layer, e[w], 0, 0)),
                      pl.BlockSpec((1, 1, 1, N), lambda w, t, e, lo, hi: (layer, e[w], 0, 0))],
            out_specs=pl.BlockSpec((tm, n_out), lambda w, t, e, lo, hi: (t[w], 0)),
            scratch_shapes=[pltpu.VMEM((K, N), BF16)]),
        out_shape=jax.ShapeDtypeStruct((P, n_out), out_dtype),
        compiler_params=_cparams(("arbitrary",)),
    )(meta[0], meta[1], meta[2], meta[3], x, w_all, b_all.reshape(b_all.shape[0], E, 1, N))


def _combine_kernel(pos_ref, ys_ref, x_ref, w_ref, g2_ref, o_ref, buf, sem, *, n_tok):
    tm = x_ref.shape[0]
    base = pl.program_id(0) * tm

    def copy(r, k):
        p = pos_ref[k * n_tok + base + r]
        return pltpu.make_async_copy(ys_ref.at[pl.ds(p, 1)], buf.at[k, pl.ds(r, 1)], sem)

    def start(r, c):
        for k in range(TOP_K):
            copy(r, k).start()
        return c

    lax.fori_loop(0, tm, start, 0)
    for k in range(TOP_K):
        pltpu.make_async_copy(ys_ref.at[pl.ds(0, tm)], buf.at[k], sem).wait()
    wt = w_ref[...]
    y = wt[:, 0:1] * buf[0]
    for k in range(1, TOP_K):
        y = y + wt[:, k:k + 1] * buf[k]
    o_ref[...] = x_ref[...] + g2_ref[0] * y


def _combine(pos_flat, ys, x, wts_t, mod3, *, n_lat_rows, seq, nbatch):
    T, D = x.shape
    tm = GATHER_TM
    n_lat_tiles = n_lat_rows // tm
    per_b = seq // tm
    return pl.pallas_call(
        functools.partial(_combine_kernel, n_tok=T),
        grid_spec=pltpu.PrefetchScalarGridSpec(
            num_scalar_prefetch=1,
            grid=(T // tm,),
            in_specs=[pl.BlockSpec(memory_space=pl.ANY),
                      pl.BlockSpec((tm, D), lambda i, pos: (i, 0)),
                      pl.BlockSpec((tm, TOP_K), lambda i, pos: (i, 0)),
                      pl.BlockSpec((1, 1, D),
                                   lambda i, pos: (jnp.where(i < n_lat_tiles, i // per_b, nbatch) * 6 + 5, 0, 0))],
            out_specs=pl.BlockSpec((tm, D), lambda i, pos: (i, 0)),
            scratch_shapes=[pltpu.VMEM((TOP_K, tm, D), F32), pltpu.SemaphoreType.DMA(())]),
        out_shape=jax.ShapeDtypeStruct((T, D), F32),
        compiler_params=_cparams(("arbitrary",)),
    )(pos_flat, ys, x, wts_t, mod3)


def _permute_w_in(w):
    nl, d, _ = w.shape
    sizes = (ATT_Q_W, ATT_KV_W, ATT_KV_W, ML_QK_W, ML_QK_W, ML_V_W, ML_V_W, N_GATES)
    offs = np.concatenate([[0], np.cumsum(sizes)])
    q, k, v, mq, mk, mv, mo, g = [w[:, :, offs[i]:offs[i + 1]] for i in range(8)]
    main = jnp.concatenate([q, mv, mo, mq, mk, k, v], axis=2).astype(BF16)
    g_pad = jnp.concatenate([g, jnp.zeros((nl, d, LANES - N_GATES), w.dtype)], axis=2).astype(BF16)
    return main, g_pad, jnp.swapaxes(g, 1, 2).astype(BF16)


def kernel(x, c, ctx, c_ctx, w_ada, b_ada, norm1_w, norm2_w, w_in, b_gates, q_norm_w, k_norm_w, attn_sink,
           mlstm_norm_w, w_out, w_router, b_router, w_gate_up, b_gate_up, w_down, b_down):
    B, S, D = x.shape
    C = ctx.shape[1]
    depth = w_ada.shape[0]
    E = w_router.shape[-1]
    n_lat_rows = B * S
    R = n_lat_rows + B * C
    assert B < 16 and S % 1024 == 0 and C % 256 == 0 and (B * C) % 1024 == 0 and D % 128 == 0

    xall = jnp.concatenate([x.reshape(n_lat_rows, D), ctx.reshape(B * C, D)], axis=0)
    cv = jnp.zeros((16, D), F32).at[:B].set(c).at[B].set(c_ctx)
    mod = _ada_mod(cv, w_ada, b_ada)

    rope_tm = 512
    tabs = _rope_tables(S, rope_tm)
    reps = ATT_KV_W // ATT_HEAD_DIM
    gi = np.arange(ATT_KV_W) // ATT_HEAD_DIM
    bd = jnp.asarray(gi[:, None] == gi[None, :], BF16)

    w_main, w_g, w_gt = _permute_w_in(w_in)
    w_out_bf = w_out.astype(BF16)

    for l in range(depth):
        last = l == depth - 1
        mod3 = mod[l].reshape(16 * 6, 1, D)
        proj, g, gt = _inproj(xall, mod3, norm1_w[l], w_main, w_g, w_gt, l, n_lat_rows=n_lat_rows, seq=S, nbatch=B)
        q, kt = _qkrope(proj, tabs, jnp.tile(q_norm_w[l], reps).reshape(1, -1),
                        jnp.tile(k_norm_w[l], reps).reshape(1, -1), bd, n_lat_rows=n_lat_rows, seq=S)
        att = _attention(q, kt, proj, attn_sink[l], nbatch=B, seq=S, ctx_len=C, with_ctx_queries=not last)
        hf, hb = _mlstm(proj, g, gt, b_gates[l], nbatch=B, seq=S, ctx_len=C)

        T = n_lat_rows if last else R
        wr_t = w_router[l].T
        wr_hi = wr_t.astype(BF16)
        wr_lo = (wr_t - wr_hi.astype(F32)).astype(BF16)
        xmid, fx, lt = _outproj(att, hf, hb, proj, xall, mod3, mlstm_norm_w[l], norm2_w[l], w_out_bf,
                                wr_hi, wr_lo, b_router[l].reshape(E, 1), l,
                                n_rows=T, n_lat_rows=n_lat_rows, seq=S, nbatch=B)
        wts, pos, meta = _route(lt, tm=MOE_TM)
        pos_flat = pos.reshape(-1)
        xs = _dispatch(pos_flat, fx)
        hs = _expert_linear(meta, xs, w_gate_up, b_gate_up, l, swiglu=True, out_dtype=BF16)
        ys = _expert_linear(meta, hs, w_down, b_down, l, swiglu=False, out_dtype=F32)
        xall = _combine(pos_flat, ys, xmid, wts.T, mod3, n_lat_rows=n_lat_rows, seq=S, nbatch=B)

    return xall[:n_lat_rows].reshape(B, S, D)
```

```python
import functools

import numpy as np
import jax
import jax.numpy as jnp
from jax import lax
from jax.experimental import pallas as pl
from jax.experimental.pallas import tpu as pltpu

F32 = jnp.float32
BF16 = jnp.bfloat16

GRID_W = 64
EPS = 1e-6
NEG_INF = -1e30
ATT_HEADS = 16
ATT_KV_HEADS = 4
ATT_HEAD_DIM = 64
ATT_GROUP = ATT_HEADS // ATT_KV_HEADS
WINDOW = 128
ATT_BLOCK = 128
ROPE_THETA = 10000.0
ML_HEADS = 4
ML_DQK = 128
ML_DV = 256
ML_CHUNK = 128
GATE_SOFTCAP = 15.0
TOP_K = 4
SWIGLU_LIMIT = 7.0
SWIGLU_ALPHA = 1.702

ATT_Q_W = ATT_HEADS * ATT_HEAD_DIM
ATT_KV_W = ATT_KV_HEADS * ATT_HEAD_DIM
ML_QK_W = ML_HEADS * ML_DQK
ML_V_W = ML_HEADS * ML_DV
N_GATES = 4 * ML_HEADS

COL_Q = 0
COL_MV = COL_Q + ATT_Q_W
COL_MO = COL_MV + ML_V_W
COL_MQ = COL_MO + ML_V_W
COL_MK = COL_MQ + ML_QK_W
COL_K = COL_MK + ML_QK_W
COL_V = COL_K + ATT_KV_W
PROJ_W = COL_V + ATT_KV_W
LOG2E = 1.4426950408889634
Q_SCALE = ATT_HEAD_DIM ** -0.5 * LOG2E

LANES = 128
VMEM_LIMIT = 56 * 1024 * 1024

MOE_TM = 256
TOK_TM = 256


def _dot(a, b):
    return jnp.dot(a, b, preferred_element_type=F32)


def _dot_nt(a, b):
    return lax.dot_general(a, b, (((1,), (1,)), ((), ())), preferred_element_type=F32)


def _split2(a):
    hi = a.astype(BF16)
    lo = (a - hi.astype(F32)).astype(BF16)
    return hi, lo


def _cparams(sem):
    return pltpu.CompilerParams(dimension_semantics=sem, vmem_limit_bytes=VMEM_LIMIT)


def _ada_kernel(cv_ref, w_ref, b_ref, o_ref):
    cv = cv_ref[...]
    s = cv * jax.nn.sigmoid(cv)
    o_ref[0] = _dot(s.astype(BF16), w_ref[0].astype(BF16)) + b_ref[0]


def _ada_mod(cv, w_ada, b_ada):
    L, D, N = w_ada.shape
    tn = 1024
    return pl.pallas_call(
        _ada_kernel,
        grid=(L, N // tn),
        in_specs=[pl.BlockSpec((16, D), lambda l, j: (0, 0)),
                  pl.BlockSpec((1, D, tn), lambda l, j: (l, 0, j)),
                  pl.BlockSpec((1, 1, tn), lambda l, j: (l, 0, j))],
        out_specs=pl.BlockSpec((1, 16, tn), lambda l, j: (l, 0, j)),
        out_shape=jax.ShapeDtypeStruct((L, 16, N), F32),
        compiler_params=_cparams(("arbitrary", "arbitrary")),
    )(cv, w_ada, b_ada.reshape(L, 1, N))


def _inproj_kernel(x_ref, sc_ref, sh_ref, nw_ref, w_ref, wg_ref, wgt_ref, p_ref, g_ref, gt_ref, hn_ref):
    @pl.when(pl.program_id(1) == 0)
    def _():
        x = x_ref[...]
        ms = jnp.mean(x * x, axis=-1, keepdims=True)
        y = x * lax.rsqrt(ms + EPS) * nw_ref[...]
        hb = (y * (1.0 + sc_ref[0]) + sh_ref[0]).astype(BF16)
        hn_ref[...] = hb
        g_ref[...] = _dot(hb, wg_ref[0])
        gt_ref[...] = _dot_nt(wgt_ref[0], hb)

    p_ref[...] = _dot(hn_ref[...], w_ref[0]).astype(BF16)


def _inproj(xall, mod3, norm_w, w_main, w_g, w_gt, layer, *, n_lat_rows, seq, nbatch):
    R, D = xall.shape
    tm, tn = 1024, 1536
    n_lat_tiles = n_lat_rows // tm
    per_b = seq // tm

    def mod_idx(which):
        return lambda i, j: (jnp.where(i < n_lat_tiles, i // per_b, nbatch) * 6 + which, 0, 0)

    return pl.pallas_call(
        _inproj_kernel,
        grid=(R // tm, PROJ_W // tn),
        in_specs=[pl.BlockSpec((tm, D), lambda i, j: (i, 0)),
                  pl.BlockSpec((1, 1, D), mod_idx(1)),
                  pl.BlockSpec((1, 1, D), mod_idx(0)),
                  pl.BlockSpec((1, D), lambda i, j: (0, 0)),
                  pl.BlockSpec((1, D, tn), lambda i, j: (layer, 0, j)),
                  pl.BlockSpec((1, D, LANES), lambda i, j: (layer, 0, 0)),
                  pl.BlockSpec((1, N_GATES, D), lambda i, j: (layer, 0, 0))],
        out_specs=[pl.BlockSpec((tm, tn), lambda i, j: (i, j)),
                   pl.BlockSpec((tm, LANES), lambda i, j: (i, 0)),
                   pl.BlockSpec((N_GATES, tm), lambda i, j: (0, i))],
        out_shape=[jax.ShapeDtypeStruct((R, PROJ_W), BF16),
                   jax.ShapeDtypeStruct((R, LANES), F32),
                   jax.ShapeDtypeStruct((N_GATES, R), F32)],
        scratch_shapes=[pltpu.VMEM((tm, D), BF16)],
        compiler_params=_cparams(("arbitrary", "arbitrary")),
    )(xall, mod3, mod3, norm_w.reshape(1, D), w_main, w_g, w_gt)


def _norm_rope(x, w_row, cos, sa, sb, bd, scale):
    x = x.astype(F32)
    hi, lo = _split2(x * x)
    ss = _dot(hi, bd) + _dot(lo, bd)
    y = x * lax.rsqrt(ss * (1.0 / ATT_HEAD_DIM) + EPS) * w_row
    w = y.shape[-1]
    r = y * cos + pltpu.roll(y, 16, 1) * sa + pltpu.roll(y, w - 16, 1) * sb
    if scale != 1.0:
        r = r * scale
    return r


def _qkrope_kernel(q_ref, k_ref, cos_ref, sa_ref, sb_ref, qw_ref, kw_ref, bd_ref, q_out, kt_out):
    cos, sa, sb, bd = cos_ref[...], sa_ref[...], sb_ref[...], bd_ref[...]
    cw = cos.shape[-1]
    for c in range(ATT_Q_W // cw):
        q_out[:, c * cw:(c + 1) * cw] = _norm_rope(q_ref[:, c * cw:(c + 1) * cw], qw_ref[...], cos, sa, sb, bd,
                                                   Q_SCALE).astype(BF16)
    kt_out[...] = _norm_rope(k_ref[...], kw_ref[...], cos, sa, sb, bd, 1.0).T.astype(BF16)


def _qkrope(proj, tabs, qw, kw, bd, *, n_lat_rows, seq):
    R = proj.shape[0]
    tm = 512
    n_lat_tiles = n_lat_rows // tm
    per_b = seq // tm
    cw = ATT_KV_W
    tab_spec = pl.BlockSpec((tm, cw), lambda i: (jnp.where(i < n_lat_tiles, i % per_b, per_b), 0))
    const = lambda i: (0, 0)
    return pl.pallas_call(
        _qkrope_kernel,
        grid=(R // tm,),
        in_specs=[pl.BlockSpec((tm, ATT_Q_W), lambda i: (i, 0)),
                  pl.BlockSpec((tm, ATT_KV_W), lambda i: (i, COL_K // ATT_KV_W)),
                  tab_spec, tab_spec, tab_spec,
                  pl.BlockSpec((1, cw), const), pl.BlockSpec((1, cw), const), pl.BlockSpec((cw, cw), const)],
        out_specs=[pl.BlockSpec((tm, ATT_Q_W), lambda i: (i, 0)),
                   pl.BlockSpec((ATT_KV_W, tm), lambda i: (0, i))],
        out_shape=[jax.ShapeDtypeStruct((R, ATT_Q_W), BF16), jax.ShapeDtypeStruct((ATT_KV_W, R), BF16)],
        compiler_params=_cparams(("arbitrary",)),
    )(proj, proj, tabs[0], tabs[1], tabs[2], qw, kw, bd)


def _rope_tables(seq, tm):
    rows = seq // GRID_W
    row = jnp.repeat(jnp.arange(rows, dtype=F32), GRID_W)
    col = jnp.tile(jnp.arange(GRID_W, dtype=F32), rows)
    half = ATT_HEAD_DIM // 2
    inv_freq = ROPE_THETA ** (-jnp.arange(0, half, 2, dtype=F32) / half)

    def ang(p):
        a = p[:, None] * inv_freq[None, :]
        return jnp.concatenate([a, a], axis=-1)

    a = jnp.concatenate([ang(row), ang(col)], axis=-1)
    cos, sin = jnp.cos(a), jnp.sin(a)
    second = (jnp.arange(ATT_HEAD_DIM) % half) >= (half // 2)
    sa = jnp.where(second[None, :], sin, 0.0)
    sb = jnp.where(second[None, :], 0.0, -sin)
    reps = ATT_KV_W // ATT_HEAD_DIM

    def fin(t, fill):
        t = jnp.tile(t, (1, reps))
        return jnp.concatenate([t, jnp.full((tm, t.shape[1]), fill, F32)], axis=0)

    return fin(cos, 1.0), fin(sa, 0.0), fin(sb, 0.0)


def _attn_kernel(sink_ref, q_ref, kp_ref, kc_ref, kn_ref, vp_ref, vc_ref, vn_ref, kx_ref, vx_ref, o_ref, *, n_lat):
    n = pl.program_id(1)
    blk, hd = ATT_BLOCK, ATT_HEAD_DIM
    n_loc = 3 * blk
    kt = jnp.concatenate([kp_ref[...], kc_ref[...], kn_ref[...], kx_ref[...]], axis=1)
    v = jnp.concatenate([vp_ref[...], vc_ref[...], vn_ref[...], vx_ref[...]], axis=0)
    nk = kt.shape[1]
    row = lax.broadcasted_iota(jnp.int32, (2 * blk, nk), 0)
    r = jnp.where(row >= blk, row - blk, row)
    w = lax.broadcasted_iota(jnp.int32, (2 * blk, nk), 1)
    rel = w - r
    jabs = n * blk - WINDOW + w
    ok = (w >= n_loc) | ((rel >= 0) & (rel <= 2 * WINDOW) & (jabs >= 0) & (jabs < n_lat * blk) & (n < n_lat))
    lo_half = lax.broadcasted_iota(jnp.int32, (1, 2 * hd), 1) < hd
    is_a = lax.broadcasted_iota(jnp.int32, (2 * blk, 1), 0) < blk
    zero = jnp.zeros((), BF16)
    one = jnp.ones((), BF16)
    for h in range(ATT_KV_HEADS):
        kth = kt[h * hd:(h + 1) * hd, :]
        rhs_s = jnp.concatenate([kth, kth], axis=0)
        vt = v[:, (h // 2) * 2 * hd:(h // 2 + 1) * 2 * hd]
        v_lo = vt if h % 2 == 0 else jnp.concatenate([vt[:, hd:], vt[:, :hd]], axis=1)
        rhs_v = jnp.where(lo_half, v_lo, one)
        for pr in range(ATT_GROUP // 2):
            ha = h * ATT_GROUP + 2 * pr
            qp = q_ref[:, ha * hd:(ha + 2) * hd]
            lhs = jnp.concatenate([jnp.where(lo_half, qp, zero), jnp.where(lo_half, zero, qp)], axis=0)
            s = jnp.where(ok, _dot(lhs, rhs_s), NEG_INF)
            sink = jnp.where(is_a, sink_ref[ha], sink_ref[ha + 1]) * LOG2E
            m = jnp.maximum(jnp.max(s, axis=-1, keepdims=True), sink)
            res = _dot(jnp.exp2(s - m).astype(BF16), rhs_v)
            e_sink = jnp.exp2(sink - m)
            ra, rb = res[:blk], res[blk:]
            num = jnp.where(lo_half, ra, pltpu.roll(rb, hd, 1))
            den = (jnp.where(lo_half, pltpu.roll(ra, hd, 1), rb)
                   + jnp.where(lo_half, e_sink[:blk], e_sink[blk:]))
            o_ref[:, ha * hd:(ha + 2) * hd] = (num / den).astype(BF16)


def _attention(q, kt, proj, sink, *, nbatch, seq, ctx_len, with_ctx_queries):
    blk = ATT_BLOCK
    n_lat = seq // blk
    n_ctx = ctx_len // blk
    nb = n_lat + (n_ctx if with_ctx_queries else 0)
    lat_blocks = nbatch * n_lat
    out_rows = nbatch * nb * blk
    vcol = COL_V // ATT_KV_W

    def qrow(b, n):
        return jnp.where(n < n_lat, b * n_lat + n, lat_blocks + b * n_ctx + (n - n_lat))

    def win(off):
        return lambda b, n: b * n_lat + jnp.clip(n + off, 0, n_lat - 1)

    def spec_k(off):
        f = win(off)
        return pl.BlockSpec((ATT_KV_W, blk), lambda b, n: (0, f(b, n)))

    def spec_v(off):
        f = win(off)
        return pl.BlockSpec((blk, ATT_KV_W), lambda b, n: (f(b, n), vcol))

    ctx_row = lambda b: (nbatch * seq) // ctx_len + b
    return pl.pallas_call(
        functools.partial(_attn_kernel, n_lat=n_lat),
        grid=(nbatch, nb),
        in_specs=[pl.BlockSpec(memory_space=pltpu.SMEM),
                  pl.BlockSpec((blk, ATT_Q_W), lambda b, n: (qrow(b, n), 0)),
                  spec_k(-1), spec_k(0), spec_k(1), spec_v(-1), spec_v(0), spec_v(1),
                  pl.BlockSpec((ATT_KV_W, ctx_len), lambda b, n: (0, ctx_row(b))),
                  pl.BlockSpec((ctx_len, ATT_KV_W), lambda b, n: (ctx_row(b), vcol))],
        out_specs=pl.BlockSpec((blk, ATT_Q_W), lambda b, n: (qrow(b, n), 0)),
        out_shape=jax.ShapeDtypeStruct((out_rows, ATT_Q_W), BF16),
        compiler_params=_cparams(("arbitrary", "arbitrary")),
    )(sink, q, kt, kt, kt, proj, proj, proj, kt, proj)


def _softcap(g):
    return GATE_SOFTCAP * jnp.tanh(g * (1.0 / GATE_SOFTCAP))


def _log_sigmoid(x):
    return jnp.minimum(x, 0.0) - jnp.log(1.0 + jnp.exp(-jnp.abs(x)))


def _mlstm_kernel(bg_ref, bgt_ref, gf_ref, gtf_ref, gb_ref, gtb_ref, qf_ref, kf_ref, vf_ref, qb_ref, kb_ref, vb_ref,
                  hf_ref, hb_ref, c_ref, n_ref, m_ref):
    L = ML_CHUNK

    @pl.when(pl.program_id(1) == 0)
    def _():
        c_ref[...] = jnp.zeros_like(c_ref)
        n_ref[...] = jnp.zeros_like(n_ref)
        m_ref[...] = jnp.zeros_like(m_ref)

    ri = lax.broadcasted_iota(jnp.int32, (L, L), 0)
    ci = lax.broadcasted_iota(jnp.int32, (L, L), 1)
    lower = ci <= ri
    upper = ci >= ri
    lower_b = lower.astype(BF16)
    upper_b = upper.astype(BF16)
    scale = ML_DQK ** -0.5

    dirs = ((gf_ref, gtf_ref, qf_ref, kf_ref, vf_ref, hf_ref, lower, lower_b, upper_b, L - 1),
            (gb_ref, gtb_ref, qb_ref, kb_ref, vb_ref, hb_ref, upper, upper_b, lower_b, 0))
    for d, (g_ref, gt_ref, q_ref, k_ref, v_ref, h_ref, tri, tri_b, trit_b, last) in enumerate(dirs):
        g = _softcap(g_ref[...] + bg_ref[...])
        gt = _softcap(gt_ref[...] + bgt_ref[...])
        lf_hi, lf_lo = _split2(_log_sigmoid(g))
        lft_hi, lft_lo = _split2(_log_sigmoid(gt))
        bcol_all = _dot(tri_b, lf_hi) + _dot(tri_b, lf_lo)
        brow_all = _dot(lft_hi, trit_b) + _dot(lft_lo, trit_b)
        for h in range(ML_HEADS):
            r = d * ML_HEADS + h
            ic = d * 2 * ML_HEADS + h
            fc = ic + ML_HEADS
            i_col, b_col = g[:, ic:ic + 1], bcol_all[:, fc:fc + 1]
            i_row, b_row = gt[ic:ic + 1, :], brow_all[fc:fc + 1, :]
            total = b_col[last:last + 1, :]
            m_prev = m_ref[r:r + 1, 0:1]
            inter = b_col + m_prev
            dmat = jnp.where(tri, b_col - b_row + i_row, NEG_INF)
            m_t = jnp.maximum(inter, jnp.max(dmat, axis=-1, keepdims=True))
            q = q_ref[:, h * ML_DQK:(h + 1) * ML_DQK]
            k_s = k_ref[:, h * ML_DQK:(h + 1) * ML_DQK].astype(F32) * scale
            v = v_ref[:, h * ML_DV:(h + 1) * ML_DV]
            w_intra = _dot_nt(q, k_s.astype(BF16)) * jnp.exp(dmat - m_t)
            w_inter = jnp.exp(inter - m_t)
            c_t = c_ref[r]
            n_row = n_ref[r:r + 1, :]
            num = _dot(w_intra.astype(BF16), v) + w_inter * _dot(q, c_t.astype(BF16))
            qn = jnp.sum(q.astype(F32) * n_row, axis=-1, keepdims=True)
            den = jnp.sum(w_intra, axis=-1, keepdims=True) + w_inter * qn
            h_out = num / jnp.maximum(jnp.abs(den), jnp.exp(-m_t))
            h_ref[:, h * ML_DV:(h + 1) * ML_DV] = h_out.astype(BF16)
            g_row = total - b_row + i_row
            g_col = total - b_col + i_col
            m_new = jnp.maximum(total + m_prev, jnp.max(g_row, axis=-1, keepdims=True))
            ws_col = jnp.exp(g_col - m_new)
            w_c = jnp.exp(total + m_prev - m_new)
            c_ref[r] = w_c * c_t + _dot(k_s.T.astype(BF16), (ws_col * v.astype(F32)).astype(BF16))
            n_ref[r:r + 1, :] = w_c * n_row + jnp.sum(ws_col * k_s, axis=0, keepdims=True)
            m_ref[r:r + 1, :] = jnp.broadcast_to(m_new, (1, LANES))


def _mlstm(proj, g, gt, b_gates, *, nbatch, seq, ctx_len):
    L = ML_CHUNK
    R = proj.shape[0]
    n_lat = seq // L
    n_ctx = ctx_len // L
    steps = n_ctx + n_lat
    lat_blocks = nbatch * n_lat

    def fwd(b, t):
        return jnp.where(t < n_ctx, lat_blocks + b * n_ctx + t, b * n_lat + t - n_ctx)

    def bwd(b, t):
        return jnp.where(t < n_ctx, lat_blocks + b * n_ctx + (n_ctx - 1 - t), b * n_lat + (steps - 1 - t))

    def rows(f, width, colblk):
        return pl.BlockSpec((L, width), lambda b, t: (f(b, t), colblk))

    def gt_spec(f):
        return pl.BlockSpec((N_GATES, L), lambda b, t: (0, f(b, t)))

    bg = jnp.zeros((1, LANES), F32).at[0, :N_GATES].set(b_gates)
    bgt = b_gates.reshape(N_GATES, 1)
    const = lambda b, t: (0, 0)
    qc, kc, vc = COL_MQ // ML_QK_W, COL_MK // ML_QK_W, COL_MV // ML_V_W
    return pl.pallas_call(
        _mlstm_kernel,
        grid=(nbatch, steps),
        in_specs=[pl.BlockSpec((1, LANES), const), pl.BlockSpec((N_GATES, 1), const),
                  rows(fwd, LANES, 0), gt_spec(fwd), rows(bwd, LANES, 0), gt_spec(bwd),
                  rows(fwd, ML_QK_W, qc), rows(fwd, ML_QK_W, kc), rows(fwd, ML_V_W, vc),
                  rows(bwd, ML_QK_W, qc), rows(bwd, ML_QK_W, kc), rows(bwd, ML_V_W, vc)],
        out_specs=[rows(fwd, ML_V_W, 0), rows(bwd, ML_V_W, 0)],
        out_shape=[jax.ShapeDtypeStruct((R, ML_V_W), BF16), jax.ShapeDtypeStruct((R, ML_V_W), BF16)],
        scratch_shapes=[pltpu.VMEM((2 * ML_HEADS, ML_DQK, ML_DV), F32),
                        pltpu.VMEM((2 * ML_HEADS, ML_DQK), F32),
                        pltpu.VMEM((2 * ML_HEADS, LANES), F32)],
        compiler_params=_cparams(("arbitrary", "arbitrary")),
    )(bg, bgt, g, gt, g, gt, proj, proj, proj, proj, proj, proj)


def _outproj_kernel(att_ref, hf_ref, hb_ref, mo_ref, x_ref, g1_ref, sc2_ref, sh2_ref, mnw_ref, n2w_ref,
                    wo_ref, wrh_ref, wrl_ref, br_ref, xo_ref, fx_ref, lt_ref):
    hsum = hf_ref[...].astype(F32) + hb_ref[...].astype(F32)
    parts = []
    for h in range(ML_HEADS):
        hh = hsum[:, h * ML_DV:(h + 1) * ML_DV]
        ms = jnp.mean(hh * hh, axis=-1, keepdims=True)
        parts.append(hh * lax.rsqrt(ms + EPS))
    ml = jnp.concatenate(parts, axis=1) * mnw_ref[...] * jax.nn.sigmoid(mo_ref[...].astype(F32))
    y = _dot(att_ref[...], wo_ref[0, :ATT_Q_W, :]) + _dot(ml.astype(BF16), wo_ref[0, ATT_Q_W:, :])
    x = x_ref[...] + g1_ref[0] * y
    xo_ref[...] = x
    ms = jnp.mean(x * x, axis=-1, keepdims=True)
    fx = (x * lax.rsqrt(ms + EPS) * n2w_ref[...]) * (1.0 + sc2_ref[0]) + sh2_ref[0]
    fx_ref[...] = fx
    f_hi, f_lo = _split2(fx)
    wrh = wrh_ref[...]
    lt_ref[...] = _dot_nt(wrh, f_hi) + _dot_nt(wrh, f_lo) + _dot_nt(wrl_ref[...], f_hi) + br_ref[...]


def _outproj(att, hf, hb, proj, xall, mod3, mnw, n2w, wo, wr_hi, wr_lo, br, layer, *, n_rows, n_lat_rows, seq,
             nbatch):
    D = xall.shape[1]
    E = wr_hi.shape[0]
    tm = TOK_TM
    n_lat_tiles = n_lat_rows // tm
    per_b = seq // tm

    def mod_idx(which):
        return lambda i: (jnp.where(i < n_lat_tiles, i // per_b, nbatch) * 6 + which, 0, 0)

    row = lambda i: (i, 0)
    const = lambda i: (0, 0)
    return pl.pallas_call(
        _outproj_kernel,
        grid=(n_rows // tm,),
        in_specs=[pl.BlockSpec((tm, ATT_Q_W), row), pl.BlockSpec((tm, ML_V_W), row), pl.BlockSpec((tm, ML_V_W), row),
                  pl.BlockSpec((tm, ML_V_W), lambda i: (i, COL_MO // ML_V_W)),
                  pl.BlockSpec((tm, D), row),
                  pl.BlockSpec((1, 1, D), mod_idx(2)), pl.BlockSpec((1, 1, D), mod_idx(4)),
                  pl.BlockSpec((1, 1, D), mod_idx(3)),
                  pl.BlockSpec((1, ML_V_W), const), pl.BlockSpec((1, D), const),
                  pl.BlockSpec((1, ATT_Q_W + ML_V_W, D), lambda i: (layer, 0, 0)),
                  pl.BlockSpec((E, D), const), pl.BlockSpec((E, D), const), pl.BlockSpec((E, 1), const)],
        out_specs=[pl.BlockSpec((tm, D), row), pl.BlockSpec((tm, D), row), pl.BlockSpec((E, tm), lambda i: (0, i))],
        out_shape=[jax.ShapeDtypeStruct((n_rows, D), F32), jax.ShapeDtypeStruct((n_rows, D), F32),
                   jax.ShapeDtypeStruct((E, n_rows), F32)],
        compiler_params=_cparams(("arbitrary",)),
    )(att, hf, hb, proj, xall, mod3, mod3, mod3, mnw.reshape(1, ML_V_W), n2w.reshape(1, D), wo, wr_hi, wr_lo, br)


def _route_kernel(lt_ref, wts_ref, pos_ref, meta_ref, init_ref, idx_ref, *, chunk, sub, tm):
    E, T = lt_ref.shape
    eio = lax.broadcasted_iota(jnp.int32, (E, chunk), 0)
    ui = lax.broadcasted_iota(jnp.int32, (sub, sub), 0)
    uj = lax.broadcasted_iota(jnp.int32, (sub, sub), 1)
    strict_upper = (ui < uj).astype(BF16)

    carry = jnp.zeros((E, 1), F32)
    for c in range(T // chunk):
        sl = slice(c * chunk, (c + 1) * chunk)
        cur = lt_ref[:, sl]
        member = jnp.zeros((E, chunk), F32)
        vals, hots = [], []
        for k in range(TOP_K):
            mx = jnp.max(cur, axis=0, keepdims=True)
            idx = jnp.min(jnp.where(cur == mx, eio, E), axis=0, keepdims=True)
            hot = eio == idx
            cur = jnp.where(hot, -jnp.inf, cur)
            member = member + hot.astype(F32)
            idx_ref[k:k + 1, sl] = idx
            vals.append(mx)
            hots.append(hot)
        ex = [jnp.exp(v - vals[0]) for v in vals]
        tot = ex[0] + ex[1] + ex[2] + ex[3]
        for k in range(TOP_K):
            wts_ref[k:k + 1, sl] = ex[k] / tot
        ranks = []
        for s in range(chunk // sub):
            mb = member[:, s * sub:(s + 1) * sub]
            ranks.append(_dot(mb.astype(BF16), strict_upper) + carry)
            carry = carry + jnp.sum(mb, axis=1, keepdims=True)
        rank = jnp.concatenate(ranks, axis=1)
        for k in range(TOP_K):
            pos_ref[k:k + 1, sl] = jnp.sum(jnp.where(hots[k], rank, 0.0), axis=0, keepdims=True).astype(jnp.int32)

    ei = lax.broadcasted_iota(jnp.int32, (E, E), 0)
    ej = lax.broadcasted_iota(jnp.int32, (E, E), 1)

    def to_row(col):
        return jnp.sum(jnp.where(ei == ej, col, 0.0), axis=0, keepdims=True)

    def excl_cumsum_col(row):
        return jnp.sum(jnp.where(ej < ei, row, 0.0), axis=1, keepdims=True)

    cnt = carry
    inv = 1.0 / tm
    padded = jnp.floor((cnt + (tm - 1.0)) * inv) * tm
    start = excl_cumsum_col(to_row(padded))
    end = start + padded
    n_used = jnp.sum(padded, axis=0, keepdims=True) * inv

    for c in range(T // chunk):
        sl = slice(c * chunk, (c + 1) * chunk)
        for k in range(TOP_K):
            off = jnp.sum(jnp.where(eio == idx_ref[k:k + 1, sl], start, 0.0), axis=0, keepdims=True)
            pos_ref[k:k + 1, sl] = pos_ref[k:k + 1, sl] + off.astype(jnp.int32) + tm

    nw = meta_ref.shape[1]
    wi = lax.broadcasted_iota(jnp.int32, (E, nw), 1).astype(F32)
    wv = jnp.minimum(wi, n_used - 1.0)
    te = jnp.sum((end <= wv * tm).astype(F32), axis=0, keepdims=True)
    nu = jnp.broadcast_to(n_used, (1, nw))
    zeros = jnp.zeros((1, nw), F32)
    meta_ref[...] = jnp.concatenate([te, nu, zeros, zeros, zeros, zeros, zeros, zeros], axis=0).astype(jnp.int32)

    ns = init_ref.shape[0]
    tile = lax.broadcasted_iota(jnp.int32, (ns, 1), 0).astype(F32) - 1.0
    tv = jnp.clip(tile, 0.0, n_used - 1.0)
    te_col = jnp.sum((to_row(end) <= tv * tm).astype(F32), axis=1, keepdims=True)
    trash_e = jnp.where((tile < 0.0) | (tile >= n_used), float(E), te_col)
    lane = lax.broadcasted_iota(jnp.int32, (ns, tm), 1).astype(F32)
    init_ref[...] = (float(TOP_K * T) + trash_e * tm + lane).astype(jnp.int32)


def _route(lt, *, tm):
    E, T = lt.shape
    nt = T * TOP_K // tm + E
    nw_pad = -(-(nt + 1) // LANES) * LANES
    return pl.pallas_call(
        functools.partial(_route_kernel, chunk=2048 if T % 2048 == 0 else 1024, sub=512, tm=tm),
        out_shape=[jax.ShapeDtypeStruct((TOP_K, T), F32), jax.ShapeDtypeStruct((TOP_K, T), jnp.int32),
                   jax.ShapeDtypeStruct((8, nw_pad), jnp.int32), jax.ShapeDtypeStruct((nt + 2, tm), jnp.int32)],
        scratch_shapes=[pltpu.VMEM((TOP_K, T), jnp.int32)],
        compiler_params=pltpu.CompilerParams(vmem_limit_bytes=VMEM_LIMIT),
    )(lt)


def _invert_kernel(pos_ref, init_ref, code_ref, sem, *, n_tok):
    cp = pltpu.make_async_copy(init_ref, code_ref, sem)
    cp.start()
    cp.wait()

    def body(t, c):
        for k in range(TOP_K):
            code_ref[pos_ref[k * n_tok + t]] = k * n_tok + t
        return c

    lax.fori_loop(0, n_tok, body, 0)


def _invert(pos_flat, init_flat):
    n_tok = pos_flat.shape[0] // TOP_K
    return pl.pallas_call(
        functools.partial(_invert_kernel, n_tok=n_tok),
        grid_spec=pltpu.PrefetchScalarGridSpec(
            num_scalar_prefetch=1,
            grid=(1,),
            in_specs=[pl.BlockSpec(memory_space=pl.ANY)],
            out_specs=pl.BlockSpec(memory_space=pltpu.SMEM),
            scratch_shapes=[pltpu.SemaphoreType.DMA(())]),
        out_shape=jax.ShapeDtypeStruct(init_flat.shape, jnp.int32),
        compiler_params=_cparams(("arbitrary",)),
    )(pos_flat, init_flat)


def _cast_expert_weight(w_ref, wbf_ref):
    rows_per = min(256, wbf_ref.shape[0])

    def cast(j, c):
        sl = pl.ds(pl.multiple_of(j * rows_per, rows_per), rows_per)
        wbf_ref[sl, :] = w_ref[0, 0, sl, :].astype(BF16)
        return c

    lax.fori_loop(0, wbf_ref.shape[0] // rows_per, cast, 0)


def _up_kernel(te_ref, nu_ref, code_ref, fx_ref, w_ref, b_ref, h_ref, xa_ref, xb_ref, wbf_ref, sem, *, n_tok):
    i = pl.program_id(0)
    n_used = nu_ref[0]
    tm = xa_ref.shape[0]
    bufs = (xa_ref, xb_ref)

    def gather(tile, par):
        base = (tile + 1) * tm
        for r in range(tm):
            c = code_ref[base + r]
            k = sum((c >= j * n_tok).astype(jnp.int32) for j in range(1, TOP_K))
            t = jnp.minimum(c - k * n_tok, n_tok - 1)
            pltpu.make_async_copy(fx_ref.at[pl.ds(t, 1)], bufs[par].at[pl.ds(r, 1)], sem.at[par]).start()

    def wait(par):
        pltpu.make_async_copy(fx_ref.at[pl.ds(0, tm)], bufs[par], sem.at[par]).wait()

    @pl.when(i == 0)
    def _():
        gather(0, 0)

    @pl.when((i < n_used) & ((i == 0) | (te_ref[i] != te_ref[jnp.maximum(i - 1, 0)])))
    def _():
        _cast_expert_weight(w_ref, wbf_ref)

    for par in range(2):
        @pl.when((i < n_used) & (i % 2 == par))
        def _():
            wait(par)
            gather(i + 1, 1 - par)
            y = _dot(bufs[par][...].astype(BF16), wbf_ref[...]) + b_ref[0, 0]
            f = y.shape[1] // 2
            gate = jnp.minimum(y[:, :f], SWIGLU_LIMIT)
            up = jnp.clip(y[:, f:], -SWIGLU_LIMIT, SWIGLU_LIMIT)
            h_ref[...] = ((up + 1.0) * gate * jax.nn.sigmoid(SWIGLU_ALPHA * gate)).astype(h_ref.dtype)

        @pl.when((i == n_used) & (i % 2 == par))
        def _():
            wait(par)

    @pl.when((i >= n_used) & (i < pl.num_programs(0) - 1))
    def _():
        h_ref[...] = jnp.zeros_like(h_ref)


def _expert_up(meta, code, fx, w_all, b_all, layer):
    T, D = fx.shape
    _, E, _, N = w_all.shape
    tm = MOE_TM
    nt = T * TOP_K // tm + E
    widx = lambda i, te, nu, code: (layer, te[i], 0, 0)
    return pl.pallas_call(
        functools.partial(_up_kernel, n_tok=T),
        grid_spec=pltpu.PrefetchScalarGridSpec(
            num_scalar_prefetch=3,
            grid=(nt + 1,),
            in_specs=[pl.BlockSpec(memory_space=pl.ANY),
                      pl.BlockSpec((1, 1, D, N), widx),
                      pl.BlockSpec((1, 1, 1, N), widx)],
            out_specs=pl.BlockSpec((tm, N // 2), lambda i, te, nu, code: (jnp.minimum(i, nt - 1), 0)),
            scratch_shapes=[pltpu.VMEM((tm, D), F32), pltpu.VMEM((tm, D), F32), pltpu.VMEM((D, N), BF16),
                            pltpu.SemaphoreType.DMA((2,))]),
        out_shape=jax.ShapeDtypeStruct((nt * tm, N // 2), BF16),
        compiler_params=_cparams(("arbitrary",)),
    )(meta[0], meta[1], code, fx, w_all, b_all.reshape(b_all.shape[0], E, 1, N))


def _down_kernel(te_ref, nu_ref, code_ref, h_ref, w_ref, b_ref, yk_ref, ya_ref, yb_ref, wbf_ref, sem, *, n_tok):
    i = pl.program_id(0)
    n_used = nu_ref[0]
    tm = ya_ref.shape[0]
    bufs = (ya_ref, yb_ref)

    def scatter(tile, par):
        base = (tile + 1) * tm
        for r in range(tm):
            dst = code_ref[base + r]
            pltpu.make_async_copy(bufs[par].at[pl.ds(r, 1)], yk_ref.at[pl.ds(dst, 1)], sem.at[par]).start()

    def wait(par):
        pltpu.make_async_copy(bufs[par], yk_ref.at[pl.ds(0, tm)], sem.at[par]).wait()

    @pl.when(i == 0)
    def _():
        yb_ref[...] = jnp.zeros_like(yb_ref)
        n_trash = (yk_ref.shape[0] - TOP_K * n_tok) // tm
        fills = [pltpu.make_async_copy(yb_ref, yk_ref.at[pl.ds(TOP_K * n_tok + j * tm, tm)], sem.at[1])
                 for j in range(n_trash)]
        for cp in fills:
            cp.start()
        for cp in fills:
            cp.wait()

    @pl.when((i < n_used) & ((i == 0) | (te_ref[i] != te_ref[jnp.maximum(i - 1, 0)])))
    def _():
        _cast_expert_weight(w_ref, wbf_ref)

    for par in range(2):
        @pl.when((i < n_used) & (i % 2 == par))
        def _():
            @pl.when(i >= 1)
            def _():
                wait(par)
            scatter(i - 1, 1 - par)
            bufs[par][...] = _dot(h_ref[...], wbf_ref[...]) + b_ref[0, 0]

        @pl.when((i == n_used) & (i % 2 == par))
        def _():
            scatter(i - 1, 1 - par)
            wait(par)
            wait(1 - par)


def _expert_down(meta, code, hs, w_all, b_all, layer, *, n_tok):
    _, E, F, D = w_all.shape
    tm = MOE_TM
    nt = hs.shape[0] // tm
    widx = lambda i, te, nu, code: (layer, te[i], 0, 0)
    return pl.pallas_call(
        functools.partial(_down_kernel, n_tok=n_tok),
        grid_spec=pltpu.PrefetchScalarGridSpec(
            num_scalar_prefetch=3,
            grid=(nt + 1,),
            in_specs=[pl.BlockSpec((tm, F), lambda i, te, nu, code: (jnp.minimum(i, nu[0] - 1), 0)),
                      pl.BlockSpec((1, 1, F, D), widx),
                      pl.BlockSpec((1, 1, 1, D), widx)],
            out_specs=pl.BlockSpec(memory_space=pl.ANY),
            scratch_shapes=[pltpu.VMEM((tm, D), F32), pltpu.VMEM((tm, D), F32), pltpu.VMEM((F, D), BF16),
                            pltpu.SemaphoreType.DMA((2,))]),
        out_shape=jax.ShapeDtypeStruct((TOP_K * n_tok + (E + 1) * tm, D), F32),
        compiler_params=_cparams(("arbitrary",)),
    )(meta[0], meta[1], code, hs, w_all, b_all.reshape(b_all.shape[0], E, 1, D))


def _combine_kernel(y0_ref, y1_ref, y2_ref, y3_ref, x_ref, w_ref, g2_ref, o_ref):
    wt = w_ref[...]
    y = wt[:, 0:1] * y0_ref[...]
    for k, y_ref in enumerate((y1_ref, y2_ref, y3_ref), start=1):
        y = y + wt[:, k:k + 1] * y_ref[...]
    o_ref[...] = x_ref[...] + g2_ref[0] * y


def _combine(yk, x, wts_t, mod3, *, n_lat_rows, seq, nbatch):
    T, D = x.shape
    tm = TOK_TM
    n_lat_tiles = n_lat_rows // tm
    per_b = seq // tm
    stride = T // tm
    planes = [pl.BlockSpec((tm, D), functools.partial(lambda i, k: (k * stride + i, 0), k=k)) for k in range(TOP_K)]
    return pl.pallas_call(
        _combine_kernel,
        grid=(T // tm,),
        in_specs=planes + [pl.BlockSpec((tm, D), lambda i: (i, 0)),
                           pl.BlockSpec((tm, TOP_K), lambda i: (i, 0)),
                           pl.BlockSpec((1, 1, D),
                                        lambda i: (jnp.where(i < n_lat_tiles, i // per_b, nbatch) * 6 + 5, 0, 0))],
        out_specs=pl.BlockSpec((tm, D), lambda i: (i, 0)),
        out_shape=jax.ShapeDtypeStruct((T, D), F32),
        compiler_params=_cparams(("arbitrary",)),
    )(yk, yk, yk, yk, x, wts_t, mod3)


def _permute_w_in(w):
    nl, d, _ = w.shape
    sizes = (ATT_Q_W, ATT_KV_W, ATT_KV_W, ML_QK_W, ML_QK_W, ML_V_W, ML_V_W, N_GATES)
    offs = np.concatenate([[0], np.cumsum(sizes)])
    q, k, v, mq, mk, mv, mo, g = [w[:, :, offs[i]:offs[i + 1]] for i in range(8)]
    main = jnp.concatenate([q, mv, mo, mq, mk, k, v], axis=2).astype(BF16)
    g_pad = jnp.concatenate([g, jnp.zeros((nl, d, LANES - N_GATES), w.dtype)], axis=2).astype(BF16)
    return main, g_pad, jnp.swapaxes(g, 1, 2).astype(BF16)


def kernel(x, c, ctx, c_ctx, w_ada, b_ada, norm1_w, norm2_w, w_in, b_gates, q_norm_w, k_norm_w, attn_sink,
           mlstm_norm_w, w_out, w_router, b_router, w_gate_up, b_gate_up, w_down, b_down):
    B, S, D = x.shape
    C = ctx.shape[1]
    depth = w_ada.shape[0]
    E = w_router.shape[-1]
    n_lat_rows = B * S
    R = n_lat_rows + B * C
    assert B < 16 and S % 1024 == 0 and C % 256 == 0 and (B * C) % 1024 == 0 and D % 128 == 0

    xall = jnp.concatenate([x.reshape(n_lat_rows, D), ctx.reshape(B * C, D)], axis=0)
    cv = jnp.zeros((16, D), F32).at[:B].set(c).at[B].set(c_ctx)
    mod = _ada_mod(cv, w_ada, b_ada)

    rope_tm = 512
    tabs = _rope_tables(S, rope_tm)
    reps = ATT_KV_W // ATT_HEAD_DIM
    gi = np.arange(ATT_KV_W) // ATT_HEAD_DIM
    bd = jnp.asarray(gi[:, None] == gi[None, :], BF16)

    w_main, w_g, w_gt = _permute_w_in(w_in)
    w_out_bf = w_out.astype(BF16)

    for l in range(depth):
        last = l == depth - 1
        mod3 = mod[l].reshape(16 * 6, 1, D)
        proj, g, gt = _inproj(xall, mod3, norm1_w[l], w_main, w_g, w_gt, l, n_lat_rows=n_lat_rows, seq=S, nbatch=B)
        q, kt = _qkrope(proj, tabs, jnp.tile(q_norm_w[l], reps).reshape(1, -1),
                        jnp.tile(k_norm_w[l], reps).reshape(1, -1), bd, n_lat_rows=n_lat_rows, seq=S)
        att = _attention(q, kt, proj, attn_sink[l], nbatch=B, seq=S, ctx_len=C, with_ctx_queries=not last)
        hf, hb = _mlstm(proj, g, gt, b_gates[l], nbatch=B, seq=S, ctx_len=C)

        T = n_lat_rows if last else R
        wr_t = w_router[l].T
        wr_hi = wr_t.astype(BF16)
        wr_lo = (wr_t - wr_hi.astype(F32)).astype(BF16)
        xmid, fx, lt = _outproj(att, hf, hb, proj, xall, mod3, mlstm_norm_w[l], norm2_w[l], w_out_bf,
                                wr_hi, wr_lo, b_router[l].reshape(E, 1), l,
                                n_rows=T, n_lat_rows=n_lat_rows, seq=S, nbatch=B)
        wts, pos, meta, code_init = _route(lt, tm=MOE_TM)
        code = _invert(pos.reshape(-1), code_init.reshape(-1))
        hs = _expert_up(meta, code, fx, w_gate_up, b_gate_up, l)
        yk = _expert_down(meta, code, hs, w_down, b_down, l, n_tok=T)
        xall = _combine(yk, xmid, wts.T, mod3, n_lat_rows=n_lat_rows, seq=S, nbatch=B)

    return xall[:n_lat_rows].reshape(B, S, D)
```

```python
import functools

import numpy as np
import jax
import jax.numpy as jnp
from jax import lax
from jax.experimental import pallas as pl
from jax.experimental.pallas import tpu as pltpu

F32 = jnp.float32
BF16 = jnp.bfloat16

GRID_W = 64
EPS = 1e-6
NEG_INF = -1e30
ATT_HEADS = 16
ATT_KV_HEADS = 4
ATT_HEAD_DIM = 64
ATT_GROUP = ATT_HEADS // ATT_KV_HEADS
WINDOW = 128
ATT_BLOCK = 128
ROPE_THETA = 10000.0
ML_HEADS = 4
ML_DQK = 128
ML_DV = 256
ML_CHUNK = 128
GATE_SOFTCAP = 15.0
TOP_K = 4
SWIGLU_LIMIT = 7.0
SWIGLU_ALPHA = 1.702

ATT_Q_W = ATT_HEADS * ATT_HEAD_DIM
ATT_KV_W = ATT_KV_HEADS * ATT_HEAD_DIM
ML_QK_W = ML_HEADS * ML_DQK
ML_V_W = ML_HEADS * ML_DV
N_GATES = 4 * ML_HEADS

COL_Q = 0
COL_MV = COL_Q + ATT_Q_W
COL_MO = COL_MV + ML_V_W
COL_MQ = COL_MO + ML_V_W
COL_MK = COL_MQ + ML_QK_W
COL_K = COL_MK + ML_QK_W
COL_V = COL_K + ATT_KV_W
PROJ_W = COL_V + ATT_KV_W
LOG2E = 1.4426950408889634
Q_SCALE = ATT_HEAD_DIM ** -0.5 * LOG2E

LANES = 128
VMEM_LIMIT = 56 * 1024 * 1024

MOE_TM = 256
TOK_TM = 256


def _dot(a, b):
    return jnp.dot(a, b, preferred_element_type=F32)


def _dot_nt(a, b):
    return lax.dot_general(a, b, (((1,), (1,)), ((), ())), preferred_element_type=F32)


def _split2(a):
    hi = a.astype(BF16)
    lo = (a - hi.astype(F32)).astype(BF16)
    return hi, lo


def _cparams(sem):
    return pltpu.CompilerParams(dimension_semantics=sem, vmem_limit_bytes=VMEM_LIMIT)


def _ada_kernel(cv_ref, w_ref, b_ref, o_ref):
    cv = cv_ref[...]
    s = cv * jax.nn.sigmoid(cv)
    o_ref[0] = _dot(s.astype(BF16), w_ref[0].astype(BF16)) + b_ref[0]


def _ada_mod(cv, w_ada, b_ada):
    L, D, N = w_ada.shape
    tn = 1024
    return pl.pallas_call(
        _ada_kernel,
        grid=(L, N // tn),
        in_specs=[pl.BlockSpec((16, D), lambda l, j: (0, 0)),
                  pl.BlockSpec((1, D, tn), lambda l, j: (l, 0, j)),
                  pl.BlockSpec((1, 1, tn), lambda l, j: (l, 0, j))],
        out_specs=pl.BlockSpec((1, 16, tn), lambda l, j: (l, 0, j)),
        out_shape=jax.ShapeDtypeStruct((L, 16, N), F32),
        compiler_params=_cparams(("arbitrary", "arbitrary")),
    )(cv, w_ada, b_ada.reshape(L, 1, N))


def _inproj_kernel(x_ref, sc_ref, sh_ref, nw_ref, w_ref, wg_ref, wgt_ref, p_ref, g_ref, gt_ref, hn_ref):
    @pl.when(pl.program_id(1) == 0)
    def _():
        x = x_ref[...]
        ms = jnp.mean(x * x, axis=-1, keepdims=True)
        y = x * lax.rsqrt(ms + EPS) * nw_ref[...]
        hb = (y * (1.0 + sc_ref[0]) + sh_ref[0]).astype(BF16)
        hn_ref[...] = hb
        g_ref[...] = _dot(hb, wg_ref[0])
        gt_ref[...] = _dot_nt(wgt_ref[0], hb)

    p_ref[...] = _dot(hn_ref[...], w_ref[0]).astype(BF16)


def _inproj(xall, mod3, norm_w, w_main, w_g, w_gt, layer, *, n_lat_rows, seq, nbatch):
    R, D = xall.shape
    tm, tn = 1024, 1536
    n_lat_tiles = n_lat_rows // tm
    per_b = seq // tm

    def mod_idx(which):
        return lambda i, j: (jnp.where(i < n_lat_tiles, i // per_b, nbatch) * 6 + which, 0, 0)

    return pl.pallas_call(
        _inproj_kernel,
        grid=(R // tm, PROJ_W // tn),
        in_specs=[pl.BlockSpec((tm, D), lambda i, j: (i, 0)),
                  pl.BlockSpec((1, 1, D), mod_idx(1)),
                  pl.BlockSpec((1, 1, D), mod_idx(0)),
                  pl.BlockSpec((1, D), lambda i, j: (0, 0)),
                  pl.BlockSpec((1, D, tn), lambda i, j: (layer, 0, j)),
                  pl.BlockSpec((1, D, LANES), lambda i, j: (layer, 0, 0)),
                  pl.BlockSpec((1, N_GATES, D), lambda i, j: (layer, 0, 0))],
        out_specs=[pl.BlockSpec((tm, tn), lambda i, j: (i, j)),
                   pl.BlockSpec((tm, LANES), lambda i, j: (i, 0)),
                   pl.BlockSpec((N_GATES, tm), lambda i, j: (0, i))],
        out_shape=[jax.ShapeDtypeStruct((R, PROJ_W), BF16),
                   jax.ShapeDtypeStruct((R, LANES), F32),
                   jax.ShapeDtypeStruct((N_GATES, R), F32)],
        scratch_shapes=[pltpu.VMEM((tm, D), BF16)],
        compiler_params=_cparams(("arbitrary", "arbitrary")),
    )(xall, mod3, mod3, norm_w.reshape(1, D), w_main, w_g, w_gt)


def _norm_rope(x, w_row, cos, sa, sb, bd, scale):
    x = x.astype(F32)
    hi, lo = _split2(x * x)
    ss = _dot(hi, bd) + _dot(lo, bd)
    y = x * lax.rsqrt(ss * (1.0 / ATT_HEAD_DIM) + EPS) * w_row
    w = y.shape[-1]
    r = y * cos + pltpu.roll(y, 16, 1) * sa + pltpu.roll(y, w - 16, 1) * sb
    if scale != 1.0:
        r = r * scale
    return r


def _qkrope_kernel(q_ref, k_ref, cos_ref, sa_ref, sb_ref, qw_ref, kw_ref, bd_ref, q_out, kt_out):
    cos, sa, sb, bd = cos_ref[...], sa_ref[...], sb_ref[...], bd_ref[...]
    cw = cos.shape[-1]
    for c in range(ATT_Q_W // cw):
        q_out[:, c * cw:(c + 1) * cw] = _norm_rope(q_ref[:, c * cw:(c + 1) * cw], qw_ref[...], cos, sa, sb, bd,
                                                   Q_SCALE).astype(BF16)
    kt_out[...] = _norm_rope(k_ref[...], kw_ref[...], cos, sa, sb, bd, 1.0).T.astype(BF16)


def _qkrope(proj, tabs, qw, kw, bd, *, n_lat_rows, seq):
    R = proj.shape[0]
    tm = 512
    n_lat_tiles = n_lat_rows // tm
    per_b = seq // tm
    cw = ATT_KV_W
    tab_spec = pl.BlockSpec((tm, cw), lambda i: (jnp.where(i < n_lat_tiles, i % per_b, per_b), 0))
    const = lambda i: (0, 0)
    return pl.pallas_call(
        _qkrope_kernel,
        grid=(R // tm,),
        in_specs=[pl.BlockSpec((tm, ATT_Q_W), lambda i: (i, 0)),
                  pl.BlockSpec((tm, ATT_KV_W), lambda i: (i, COL_K // ATT_KV_W)),
                  tab_spec, tab_spec, tab_spec,
                  pl.BlockSpec((1, cw), const), pl.BlockSpec((1, cw), const), pl.BlockSpec((cw, cw), const)],
        out_specs=[pl.BlockSpec((tm, ATT_Q_W), lambda i: (i, 0)),
                   pl.BlockSpec((ATT_KV_W, tm), lambda i: (0, i))],
        out_shape=[jax.ShapeDtypeStruct((R, ATT_Q_W), BF16), jax.ShapeDtypeStruct((ATT_KV_W, R), BF16)],
        compiler_params=_cparams(("arbitrary",)),
    )(proj, proj, tabs[0], tabs[1], tabs[2], qw, kw, bd)


def _rope_tables(seq, tm):
    rows = seq // GRID_W
    row = jnp.repeat(jnp.arange(rows, dtype=F32), GRID_W)
    col = jnp.tile(jnp.arange(GRID_W, dtype=F32), rows)
    half = ATT_HEAD_DIM // 2
    inv_freq = ROPE_THETA ** (-jnp.arange(0, half, 2, dtype=F32) / half)

    def ang(p):
        a = p[:, None] * inv_freq[None, :]
        return jnp.concatenate([a, a], axis=-1)

    a = jnp.concatenate([ang(row), ang(col)], axis=-1)
    cos, sin = jnp.cos(a), jnp.sin(a)
    second = (jnp.arange(ATT_HEAD_DIM) % half) >= (half // 2)
    sa = jnp.where(second[None, :], sin, 0.0)
    sb = jnp.where(second[None, :], 0.0, -sin)
    reps = ATT_KV_W // ATT_HEAD_DIM

    def fin(t, fill):
        t = jnp.tile(t, (1, reps))
        return jnp.concatenate([t, jnp.full((tm, t.shape[1]), fill, F32)], axis=0)

    return fin(cos, 1.0), fin(sa, 0.0), fin(sb, 0.0)


def _attn_kernel(sink_ref, q_ref, kp_ref, kc_ref, kn_ref, vp_ref, vc_ref, vn_ref, kx_ref, vx_ref, o_ref, *, n_lat):
    n = pl.program_id(1)
    blk, hd = ATT_BLOCK, ATT_HEAD_DIM
    n_loc = 3 * blk
    kt = jnp.concatenate([kp_ref[...], kc_ref[...], kn_ref[...], kx_ref[...]], axis=1)
    v = jnp.concatenate([vp_ref[...], vc_ref[...], vn_ref[...], vx_ref[...]], axis=0)
    row = lax.broadcasted_iota(jnp.int32, (2 * blk, n_loc), 0)
    r = jnp.where(row >= blk, row - blk, row)
    w = lax.broadcasted_iota(jnp.int32, (2 * blk, n_loc), 1)
    rel = w - r
    jabs = n * blk - WINDOW + w
    ok = (rel >= 0) & (rel <= 2 * WINDOW) & (jabs >= 0) & (jabs < n_lat * blk) & (n < n_lat)
    lo_half = lax.broadcasted_iota(jnp.int32, (1, 2 * hd), 1) < hd
    is_a = lax.broadcasted_iota(jnp.int32, (2 * blk, 1), 0) < blk
    zero = jnp.zeros((), BF16)
    one = jnp.ones((), BF16)
    for h in range(ATT_KV_HEADS):
        kth = kt[h * hd:(h + 1) * hd, :]
        rhs_s = jnp.concatenate([kth, kth], axis=0)
        vt = v[:, (h // 2) * 2 * hd:(h // 2 + 1) * 2 * hd]
        v_lo = vt if h % 2 == 0 else jnp.concatenate([vt[:, hd:], vt[:, :hd]], axis=1)
        rhs_v = jnp.where(lo_half, v_lo, one)
        for pr in range(ATT_GROUP // 2):
            ha = h * ATT_GROUP + 2 * pr
            qp = q_ref[:, ha * hd:(ha + 2) * hd]
            lhs = jnp.concatenate([jnp.where(lo_half, qp, zero), jnp.where(lo_half, zero, qp)], axis=0)
            s = _dot(lhs, rhs_s)
            s_loc = jnp.where(ok, s[:, :n_loc], NEG_INF)
            s_ctx = s[:, n_loc:]
            sink = jnp.where(is_a, sink_ref[ha], sink_ref[ha + 1]) * LOG2E
            m = jnp.maximum(jnp.maximum(jnp.max(s_loc, axis=-1, keepdims=True),
                                        jnp.max(s_ctx, axis=-1, keepdims=True)), sink)
            p = jnp.concatenate([jnp.exp2(s_loc - m), jnp.exp2(s_ctx - m)], axis=1).astype(BF16)
            res = _dot(p, rhs_v)
            e_sink = jnp.exp2(sink - m)
            ra, rb = res[:blk], res[blk:]
            num = jnp.where(lo_half, ra, pltpu.roll(rb, hd, 1))
            den = (jnp.where(lo_half, pltpu.roll(ra, hd, 1), rb)
                   + jnp.where(lo_half, e_sink[:blk], e_sink[blk:]))
            o_ref[:, ha * hd:(ha + 2) * hd] = (num / den).astype(BF16)


def _attention(q, kt, proj, sink, *, nbatch, seq, ctx_len, with_ctx_queries):
    blk = ATT_BLOCK
    n_lat = seq // blk
    n_ctx = ctx_len // blk
    nb = n_lat + (n_ctx if with_ctx_queries else 0)
    lat_blocks = nbatch * n_lat
    out_rows = nbatch * nb * blk
    vcol = COL_V // ATT_KV_W

    def qrow(b, n):
        return jnp.where(n < n_lat, b * n_lat + n, lat_blocks + b * n_ctx + (n - n_lat))

    def win(off):
        return lambda b, n: b * n_lat + jnp.clip(n + off, 0, n_lat - 1)

    def spec_k(off):
        f = win(off)
        return pl.BlockSpec((ATT_KV_W, blk), lambda b, n: (0, f(b, n)))

    def spec_v(off):
        f = win(off)
        return pl.BlockSpec((blk, ATT_KV_W), lambda b, n: (f(b, n), vcol))

    ctx_row = lambda b: (nbatch * seq) // ctx_len + b
    return pl.pallas_call(
        functools.partial(_attn_kernel, n_lat=n_lat),
        grid=(nbatch, nb),
        in_specs=[pl.BlockSpec(memory_space=pltpu.SMEM),
                  pl.BlockSpec((blk, ATT_Q_W), lambda b, n: (qrow(b, n), 0)),
                  spec_k(-1), spec_k(0), spec_k(1), spec_v(-1), spec_v(0), spec_v(1),
                  pl.BlockSpec((ATT_KV_W, ctx_len), lambda b, n: (0, ctx_row(b))),
                  pl.BlockSpec((ctx_len, ATT_KV_W), lambda b, n: (ctx_row(b), vcol))],
        out_specs=pl.BlockSpec((blk, ATT_Q_W), lambda b, n: (qrow(b, n), 0)),
        out_shape=jax.ShapeDtypeStruct((out_rows, ATT_Q_W), BF16),
        compiler_params=_cparams(("arbitrary", "arbitrary")),
    )(sink, q, kt, kt, kt, proj, proj, proj, kt, proj)


def _softcap(g):
    return GATE_SOFTCAP * jnp.tanh(g * (1.0 / GATE_SOFTCAP))


def _log_sigmoid(x):
    return jnp.minimum(x, 0.0) - jnp.log(1.0 + jnp.exp(-jnp.abs(x)))


def _mlstm_kernel(bg_ref, bgt_ref, gf_ref, gtf_ref, gb_ref, gtb_ref, qf_ref, kf_ref, vf_ref, qb_ref, kb_ref, vb_ref,
                  hf_ref, hb_ref, c_ref, n_ref, m_ref):
    L = ML_CHUNK

    @pl.when(pl.program_id(1) == 0)
    def _():
        c_ref[...] = jnp.zeros_like(c_ref)
        n_ref[...] = jnp.zeros_like(n_ref)
        m_ref[...] = jnp.zeros_like(m_ref)

    ri = lax.broadcasted_iota(jnp.int32, (L, L), 0)
    ci = lax.broadcasted_iota(jnp.int32, (L, L), 1)
    lower = ci <= ri
    upper = ci >= ri
    lower_b = lower.astype(BF16)
    upper_b = upper.astype(BF16)
    scale = ML_DQK ** -0.5

    dirs = ((gf_ref, gtf_ref, qf_ref, kf_ref, vf_ref, hf_ref, lower, lower_b, upper_b, L - 1),
            (gb_ref, gtb_ref, qb_ref, kb_ref, vb_ref, hb_ref, upper, upper_b, lower_b, 0))
    for d, (g_ref, gt_ref, q_ref, k_ref, v_ref, h_ref, tri, tri_b, trit_b, last) in enumerate(dirs):
        g = _softcap(g_ref[...] + bg_ref[...])
        gt = _softcap(gt_ref[...] + bgt_ref[...])
        lf_hi, lf_lo = _split2(_log_sigmoid(g))
        lft_hi, lft_lo = _split2(_log_sigmoid(gt))
        bcol_all = _dot(tri_b, lf_hi) + _dot(tri_b, lf_lo)
        brow_all = _dot(lft_hi, trit_b) + _dot(lft_lo, trit_b)
        for h in range(ML_HEADS):
            r = d * ML_HEADS + h
            ic = d * 2 * ML_HEADS + h
            fc = ic + ML_HEADS
            i_col, b_col = g[:, ic:ic + 1], bcol_all[:, fc:fc + 1]
            i_row, b_row = gt[ic:ic + 1, :], brow_all[fc:fc + 1, :]
            total = b_col[last:last + 1, :]
            m_prev = m_ref[r:r + 1, 0:1]
            inter = b_col + m_prev
            dmat = jnp.where(tri, b_col - b_row + i_row, NEG_INF)
            m_t = jnp.maximum(inter, jnp.max(dmat, axis=-1, keepdims=True))
            q = q_ref[:, h * ML_DQK:(h + 1) * ML_DQK]
            k_s = k_ref[:, h * ML_DQK:(h + 1) * ML_DQK].astype(F32) * scale
            v = v_ref[:, h * ML_DV:(h + 1) * ML_DV]
            w_intra = _dot_nt(q, k_s.astype(BF16)) * jnp.exp(dmat - m_t)
            w_inter = jnp.exp(inter - m_t)
            c_t = c_ref[r]
            n_row = n_ref[r:r + 1, :]
            num = _dot(w_intra.astype(BF16), v) + w_inter * _dot(q, c_t.astype(BF16))
            qn = jnp.sum(q.astype(F32) * n_row, axis=-1, keepdims=True)
            den = jnp.sum(w_intra, axis=-1, keepdims=True) + w_inter * qn
            h_out = num / jnp.maximum(jnp.abs(den), jnp.exp(-m_t))
            h_ref[:, h * ML_DV:(h + 1) * ML_DV] = h_out.astype(BF16)
            g_row = total - b_row + i_row
            g_col = total - b_col + i_col
            m_new = jnp.maximum(total + m_prev, jnp.max(g_row, axis=-1, keepdims=True))
            ws_col = jnp.exp(g_col - m_new)
            w_c = jnp.exp(total + m_prev - m_new)
            c_ref[r] = w_c * c_t + _dot(k_s.T.astype(BF16), (ws_col * v.astype(F32)).astype(BF16))
            n_ref[r:r + 1, :] = w_c * n_row + jnp.sum(ws_col * k_s, axis=0, keepdims=True)
            m_ref[r:r + 1, :] = jnp.broadcast_to(m_new, (1, LANES))


def _mlstm(proj, g, gt, b_gates, *, nbatch, seq, ctx_len):
    L = ML_CHUNK
    R = proj.shape[0]
    n_lat = seq // L
    n_ctx = ctx_len // L
    steps = n_ctx + n_lat
    lat_blocks = nbatch * n_lat

    def fwd(b, t):
        return jnp.where(t < n_ctx, lat_blocks + b * n_ctx + t, b * n_lat + t - n_ctx)

    def bwd(b, t):
        return jnp.where(t < n_ctx, lat_blocks + b * n_ctx + (n_ctx - 1 - t), b * n_lat + (steps - 1 - t))

    def rows(f, width, colblk):
        return pl.BlockSpec((L, width), lambda b, t: (f(b, t), colblk))

    def gt_spec(f):
        return pl.BlockSpec((N_GATES, L), lambda b, t: (0, f(b, t)))

    bg = jnp.zeros((1, LANES), F32).at[0, :N_GATES].set(b_gates)
    bgt = b_gates.reshape(N_GATES, 1)
    const = lambda b, t: (0, 0)
    qc, kc, vc = COL_MQ // ML_QK_W, COL_MK // ML_QK_W, COL_MV // ML_V_W
    return pl.pallas_call(
        _mlstm_kernel,
        grid=(nbatch, steps),
        in_specs=[pl.BlockSpec((1, LANES), const), pl.BlockSpec((N_GATES, 1), const),
                  rows(fwd, LANES, 0), gt_spec(fwd), rows(bwd, LANES, 0), gt_spec(bwd),
                  rows(fwd, ML_QK_W, qc), rows(fwd, ML_QK_W, kc), rows(fwd, ML_V_W, vc),
                  rows(bwd, ML_QK_W, qc), rows(bwd, ML_QK_W, kc), rows(bwd, ML_V_W, vc)],
        out_specs=[rows(fwd, ML_V_W, 0), rows(bwd, ML_V_W, 0)],
        out_shape=[jax.ShapeDtypeStruct((R, ML_V_W), BF16), jax.ShapeDtypeStruct((R, ML_V_W), BF16)],
        scratch_shapes=[pltpu.VMEM((2 * ML_HEADS, ML_DQK, ML_DV), F32),
                        pltpu.VMEM((2 * ML_HEADS, ML_DQK), F32),
                        pltpu.VMEM((2 * ML_HEADS, LANES), F32)],
        compiler_params=_cparams(("arbitrary", "arbitrary")),
    )(bg, bgt, g, gt, g, gt, proj, proj, proj, proj, proj, proj)


def _outproj_kernel(att_ref, hf_ref, hb_ref, mo_ref, x_ref, g1_ref, sc2_ref, sh2_ref, mnw_ref, n2w_ref,
                    wo_ref, wrh_ref, wrl_ref, br_ref, xo_ref, fx_ref, lt_ref):
    hsum = hf_ref[...].astype(F32) + hb_ref[...].astype(F32)
    parts = []
    for h in range(ML_HEADS):
        hh = hsum[:, h * ML_DV:(h + 1) * ML_DV]
        ms = jnp.mean(hh * hh, axis=-1, keepdims=True)
        parts.append(hh * lax.rsqrt(ms + EPS))
    ml = jnp.concatenate(parts, axis=1) * mnw_ref[...] * jax.nn.sigmoid(mo_ref[...].astype(F32))
    y = _dot(att_ref[...], wo_ref[0, :ATT_Q_W, :]) + _dot(ml.astype(BF16), wo_ref[0, ATT_Q_W:, :])
    x = x_ref[...] + g1_ref[0] * y
    xo_ref[...] = x
    ms = jnp.mean(x * x, axis=-1, keepdims=True)
    fx = (x * lax.rsqrt(ms + EPS) * n2w_ref[...]) * (1.0 + sc2_ref[0]) + sh2_ref[0]
    fx_ref[...] = fx
    f_hi, f_lo = _split2(fx)
    wrh = wrh_ref[...]
    lt_ref[...] = _dot_nt(wrh, f_hi) + _dot_nt(wrh, f_lo) + _dot_nt(wrl_ref[...], f_hi) + br_ref[...]


def _outproj(att, hf, hb, proj, xall, mod3, mnw, n2w, wo, wr_hi, wr_lo, br, layer, *, n_rows, n_lat_rows, seq,
             nbatch):
    D = xall.shape[1]
    E = wr_hi.shape[0]
    tm = TOK_TM
    n_lat_tiles = n_lat_rows // tm
    per_b = seq // tm

    def mod_idx(which):
        return lambda i: (jnp.where(i < n_lat_tiles, i // per_b, nbatch) * 6 + which, 0, 0)

    row = lambda i: (i, 0)
    const = lambda i: (0, 0)
    return pl.pallas_call(
        _outproj_kernel,
        grid=(n_rows // tm,),
        in_specs=[pl.BlockSpec((tm, ATT_Q_W), row), pl.BlockSpec((tm, ML_V_W), row), pl.BlockSpec((tm, ML_V_W), row),
                  pl.BlockSpec((tm, ML_V_W), lambda i: (i, COL_MO // ML_V_W)),
                  pl.BlockSpec((tm, D), row),
                  pl.BlockSpec((1, 1, D), mod_idx(2)), pl.BlockSpec((1, 1, D), mod_idx(4)),
                  pl.BlockSpec((1, 1, D), mod_idx(3)),
                  pl.BlockSpec((1, ML_V_W), const), pl.BlockSpec((1, D), const),
                  pl.BlockSpec((1, ATT_Q_W + ML_V_W, D), lambda i: (layer, 0, 0)),
                  pl.BlockSpec((E, D), const), pl.BlockSpec((E, D), const), pl.BlockSpec((E, 1), const)],
        out_specs=[pl.BlockSpec((tm, D), row), pl.BlockSpec((tm, D), row), pl.BlockSpec((E, tm), lambda i: (0, i))],
        out_shape=[jax.ShapeDtypeStruct((n_rows, D), F32), jax.ShapeDtypeStruct((n_rows, D), F32),
                   jax.ShapeDtypeStruct((E, n_rows), F32)],
        compiler_params=_cparams(("arbitrary",)),
    )(att, hf, hb, proj, xall, mod3, mod3, mod3, mnw.reshape(1, ML_V_W), n2w.reshape(1, D), wo, wr_hi, wr_lo, br)


def _route_kernel(lt_ref, wts_ref, pos_ref, meta_ref, init_ref, idx_ref, *, chunk, sub, tm):
    E, T = lt_ref.shape
    eio = lax.broadcasted_iota(jnp.int32, (E, chunk), 0)
    ui = lax.broadcasted_iota(jnp.int32, (sub, sub), 0)
    uj = lax.broadcasted_iota(jnp.int32, (sub, sub), 1)
    strict_upper = (ui < uj).astype(BF16)

    carry = jnp.zeros((E, 1), F32)
    for c in range(T // chunk):
        sl = slice(c * chunk, (c + 1) * chunk)
        cur = lt_ref[:, sl]
        member = jnp.zeros((E, chunk), F32)
        vals, hots = [], []
        for k in range(TOP_K):
            mx = jnp.max(cur, axis=0, keepdims=True)
            idx = jnp.min(jnp.where(cur == mx, eio, E), axis=0, keepdims=True)
            hot = eio == idx
            cur = jnp.where(hot, -jnp.inf, cur)
            member = member + hot.astype(F32)
            idx_ref[k:k + 1, sl] = idx
            vals.append(mx)
            hots.append(hot)
        ex = [jnp.exp(v - vals[0]) for v in vals]
        tot = ex[0] + ex[1] + ex[2] + ex[3]
        for k in range(TOP_K):
            wts_ref[k:k + 1, sl] = ex[k] / tot
        ranks = []
        for s in range(chunk // sub):
            mb = member[:, s * sub:(s + 1) * sub]
            ranks.append(_dot(mb.astype(BF16), strict_upper) + carry)
            carry = carry + jnp.sum(mb, axis=1, keepdims=True)
        rank = jnp.concatenate(ranks, axis=1)
        for k in range(TOP_K):
            pos_ref[k:k + 1, sl] = jnp.sum(jnp.where(hots[k], rank, 0.0), axis=0, keepdims=True).astype(jnp.int32)

    ei = lax.broadcasted_iota(jnp.int32, (E, E), 0)
    ej = lax.broadcasted_iota(jnp.int32, (E, E), 1)

    def to_row(col):
        return jnp.sum(jnp.where(ei == ej, col, 0.0), axis=0, keepdims=True)

    def excl_cumsum_col(row):
        return jnp.sum(jnp.where(ej < ei, row, 0.0), axis=1, keepdims=True)

    cnt = carry
    inv = 1.0 / tm
    padded = jnp.floor((cnt + (tm - 1.0)) * inv) * tm
    start = excl_cumsum_col(to_row(padded))
    end = start + padded
    n_used = jnp.sum(padded, axis=0, keepdims=True) * inv

    for c in range(T // chunk):
        sl = slice(c * chunk, (c + 1) * chunk)
        for k in range(TOP_K):
            off = jnp.sum(jnp.where(eio == idx_ref[k:k + 1, sl], start, 0.0), axis=0, keepdims=True)
            pos_ref[k:k + 1, sl] = pos_ref[k:k + 1, sl] + off.astype(jnp.int32) + tm

    nw = meta_ref.shape[1]
    wi = lax.broadcasted_iota(jnp.int32, (E, nw), 1).astype(F32)
    wv = jnp.minimum(wi, n_used - 1.0)
    te = jnp.sum((end <= wv * tm).astype(F32), axis=0, keepdims=True)
    nu = jnp.broadcast_to(n_used, (1, nw))
    zeros = jnp.zeros((1, nw), F32)
    meta_ref[...] = jnp.concatenate([te, nu, zeros, zeros, zeros, zeros, zeros, zeros], axis=0).astype(jnp.int32)

    ns = init_ref.shape[0]
    tile = lax.broadcasted_iota(jnp.int32, (ns, 1), 0).astype(F32) - 1.0
    tv = jnp.clip(tile, 0.0, n_used - 1.0)
    te_col = jnp.sum((to_row(end) <= tv * tm).astype(F32), axis=1, keepdims=True)
    trash_e = jnp.where((tile < 0.0) | (tile >= n_used), float(E), te_col)
    lane = lax.broadcasted_iota(jnp.int32, (ns, tm), 1).astype(F32)
    init_ref[...] = (float(TOP_K * T) + trash_e * tm + lane).astype(jnp.int32)


def _route(lt, *, tm):
    E, T = lt.shape
    nt = T * TOP_K // tm + E
    nw_pad = -(-(nt + 1) // LANES) * LANES
    return pl.pallas_call(
        functools.partial(_route_kernel, chunk=2048 if T % 2048 == 0 else 1024, sub=512, tm=tm),
        out_shape=[jax.ShapeDtypeStruct((TOP_K, T), F32), jax.ShapeDtypeStruct((TOP_K, T), jnp.int32),
                   jax.ShapeDtypeStruct((8, nw_pad), jnp.int32), jax.ShapeDtypeStruct((nt + 2, tm), jnp.int32)],
        scratch_shapes=[pltpu.VMEM((TOP_K, T), jnp.int32)],
        compiler_params=pltpu.CompilerParams(vmem_limit_bytes=VMEM_LIMIT),
    )(lt)


def _invert_kernel(pos_ref, init_ref, code_ref, sem, *, n_tok):
    cp = pltpu.make_async_copy(init_ref, code_ref, sem)
    cp.start()
    cp.wait()

    def body(t, c):
        for k in range(TOP_K):
            code_ref[pos_ref[k * n_tok + t]] = k * n_tok + t
        return c

    lax.fori_loop(0, n_tok, body, 0, unroll=4)


def _invert(pos_flat, init_flat):
    n_tok = pos_flat.shape[0] // TOP_K
    return pl.pallas_call(
        functools.partial(_invert_kernel, n_tok=n_tok),
        grid_spec=pltpu.PrefetchScalarGridSpec(
            num_scalar_prefetch=1,
            grid=(1,),
            in_specs=[pl.BlockSpec(memory_space=pl.ANY)],
            out_specs=pl.BlockSpec(memory_space=pltpu.SMEM),
            scratch_shapes=[pltpu.SemaphoreType.DMA(())]),
        out_shape=jax.ShapeDtypeStruct(init_flat.shape, jnp.int32),
        compiler_params=_cparams(("arbitrary",)),
    )(pos_flat, init_flat)


def _cast_expert_weight(w_ref, wbf_ref):
    rows_per = min(256, wbf_ref.shape[0])

    def cast(j, c):
        sl = pl.ds(pl.multiple_of(j * rows_per, rows_per), rows_per)
        wbf_ref[sl, :] = w_ref[0, 0, sl, :].astype(BF16)
        return c

    lax.fori_loop(0, wbf_ref.shape[0] // rows_per, cast, 0)


def _up_kernel(te_ref, nu_ref, code_ref, fx_ref, w_ref, b_ref, h_ref, xa_ref, xb_ref, wbf_ref, sem, *, n_tok):
    i = pl.program_id(0)
    n_used = nu_ref[0]
    tm = xa_ref.shape[0]
    bufs = (xa_ref, xb_ref)

    def gather(tile, par):
        base = (tile + 1) * tm
        for r in range(tm):
            c = code_ref[base + r]
            k = sum((c >= j * n_tok).astype(jnp.int32) for j in range(1, TOP_K))
            t = jnp.minimum(c - k * n_tok, n_tok - 1)
            pltpu.make_async_copy(fx_ref.at[pl.ds(t, 1)], bufs[par].at[pl.ds(r, 1)], sem.at[par]).start(
                priority=r % 2)

    def wait(par):
        pltpu.make_async_copy(fx_ref.at[pl.ds(0, tm)], bufs[par], sem.at[par]).wait()

    @pl.when(i == 0)
    def _():
        gather(0, 0)

    @pl.when((i < n_used) & ((i == 0) | (te_ref[i] != te_ref[jnp.maximum(i - 1, 0)])))
    def _():
        _cast_expert_weight(w_ref, wbf_ref)

    for par in range(2):
        @pl.when((i < n_used) & (i % 2 == par))
        def _():
            wait(par)
            gather(i + 1, 1 - par)
            y = _dot(bufs[par][...].astype(BF16), wbf_ref[...]) + b_ref[0, 0]
            f = y.shape[1] // 2
            gate = jnp.minimum(y[:, :f], SWIGLU_LIMIT)
            up = jnp.clip(y[:, f:], -SWIGLU_LIMIT, SWIGLU_LIMIT)
            h_ref[...] = ((up + 1.0) * gate * jax.nn.sigmoid(SWIGLU_ALPHA * gate)).astype(h_ref.dtype)

        @pl.when((i == n_used) & (i % 2 == par))
        def _():
            wait(par)

    @pl.when((i >= n_used) & (i < pl.num_programs(0) - 1))
    def _():
        h_ref[...] = jnp.zeros_like(h_ref)


def _expert_up(meta, code, fx, w_all, b_all, layer):
    T, D = fx.shape
    _, E, _, N = w_all.shape
    tm = MOE_TM
    nt = T * TOP_K // tm + E
    widx = lambda i, te, nu, code: (layer, te[i], 0, 0)
    return pl.pallas_call(
        functools.partial(_up_kernel, n_tok=T),
        grid_spec=pltpu.PrefetchScalarGridSpec(
            num_scalar_prefetch=3,
            grid=(nt + 1,),
            in_specs=[pl.BlockSpec(memory_space=pl.ANY),
                      pl.BlockSpec((1, 1, D, N), widx),
                      pl.BlockSpec((1, 1, 1, N), widx)],
            out_specs=pl.BlockSpec((tm, N // 2), lambda i, te, nu, code: (jnp.minimum(i, nt - 1), 0)),
            scratch_shapes=[pltpu.VMEM((tm, D), F32), pltpu.VMEM((tm, D), F32), pltpu.VMEM((D, N), BF16),
                            pltpu.SemaphoreType.DMA((2,))]),
        out_shape=jax.ShapeDtypeStruct((nt * tm, N // 2), BF16),
        compiler_params=_cparams(("arbitrary",)),
    )(meta[0], meta[1], code, fx, w_all, b_all.reshape(b_all.shape[0], E, 1, N))


def _down_kernel(te_ref, nu_ref, code_ref, h_ref, w_ref, b_ref, yk_ref, ya_ref, yb_ref, wbf_ref, sem, *, n_tok):
    i = pl.program_id(0)
    n_used = nu_ref[0]
    tm = ya_ref.shape[0]
    bufs = (ya_ref, yb_ref)

    def scatter(tile, par):
        base = (tile + 1) * tm
        for r in range(tm):
            dst = code_ref[base + r]
            pltpu.make_async_copy(bufs[par].at[pl.ds(r, 1)], yk_ref.at[pl.ds(dst, 1)], sem.at[par]).start(
                priority=r % 2)

    def wait(par):
        pltpu.make_async_copy(bufs[par], yk_ref.at[pl.ds(0, tm)], sem.at[par]).wait()

    @pl.when(i == 0)
    def _():
        yb_ref[...] = jnp.zeros_like(yb_ref)
        n_trash = (yk_ref.shape[0] - TOP_K * n_tok) // tm
        fills = [pltpu.make_async_copy(yb_ref, yk_ref.at[pl.ds(TOP_K * n_tok + j * tm, tm)], sem.at[1])
                 for j in range(n_trash)]
        for cp in fills:
            cp.start()
        for cp in fills:
            cp.wait()

    @pl.when((i < n_used) & ((i == 0) | (te_ref[i] != te_ref[jnp.maximum(i - 1, 0)])))
    def _():
        _cast_expert_weight(w_ref, wbf_ref)

    for par in range(2):
        @pl.when((i < n_used) & (i % 2 == par))
        def _():
            @pl.when(i >= 1)
            def _():
                wait(par)
            scatter(i - 1, 1 - par)
            bufs[par][...] = _dot(h_ref[...], wbf_ref[...]) + b_ref[0, 0]

        @pl.when((i == n_used) & (i % 2 == par))
        def _():
            scatter(i - 1, 1 - par)
            wait(par)
            wait(1 - par)


def _expert_down(meta, code, hs, w_all, b_all, layer, *, n_tok):
    _, E, F, D = w_all.shape
    tm = MOE_TM
    nt = hs.shape[0] // tm
    widx = lambda i, te, nu, code: (layer, te[i], 0, 0)
    return pl.pallas_call(
        functools.partial(_down_kernel, n_tok=n_tok),
        grid_spec=pltpu.PrefetchScalarGridSpec(
            num_scalar_prefetch=3,
            grid=(nt + 1,),
            in_specs=[pl.BlockSpec((tm, F), lambda i, te, nu, code: (jnp.minimum(i, nu[0] - 1), 0)),
                      pl.BlockSpec((1, 1, F, D), widx),
                      pl.BlockSpec((1, 1, 1, D), widx)],
            out_specs=pl.BlockSpec(memory_space=pl.ANY),
            scratch_shapes=[pltpu.VMEM((tm, D), F32), pltpu.VMEM((tm, D), F32), pltpu.VMEM((F, D), BF16),
                            pltpu.SemaphoreType.DMA((2,))]),
        out_shape=jax.ShapeDtypeStruct((TOP_K * n_tok + (E + 1) * tm, D), F32),
        compiler_params=_cparams(("arbitrary",)),
    )(meta[0], meta[1], code, hs, w_all, b_all.reshape(b_all.shape[0], E, 1, D))


def _combine_kernel(y0_ref, y1_ref, y2_ref, y3_ref, x_ref, w_ref, g2_ref, o_ref):
    wt = w_ref[...]
    y = wt[:, 0:1] * y0_ref[...]
    for k, y_ref in enumerate((y1_ref, y2_ref, y3_ref), start=1):
        y = y + wt[:, k:k + 1] * y_ref[...]
    o_ref[...] = x_ref[...] + g2_ref[0] * y


def _combine(yk, x, wts_t, mod3, *, n_lat_rows, seq, nbatch):
    T, D = x.shape
    tm = TOK_TM
    n_lat_tiles = n_lat_rows // tm
    per_b = seq // tm
    stride = T // tm
    planes = [pl.BlockSpec((tm, D), functools.partial(lambda i, k: (k * stride + i, 0), k=k)) for k in range(TOP_K)]
    return pl.pallas_call(
        _combine_kernel,
        grid=(T // tm,),
        in_specs=planes + [pl.BlockSpec((tm, D), lambda i: (i, 0)),
                           pl.BlockSpec((tm, TOP_K), lambda i: (i, 0)),
                           pl.BlockSpec((1, 1, D),
                                        lambda i: (jnp.where(i < n_lat_tiles, i // per_b, nbatch) * 6 + 5, 0, 0))],
        out_specs=pl.BlockSpec((tm, D), lambda i: (i, 0)),
        out_shape=jax.ShapeDtypeStruct((T, D), F32),
        compiler_params=_cparams(("arbitrary",)),
    )(yk, yk, yk, yk, x, wts_t, mod3)


def _permute_w_in(w):
    nl, d, _ = w.shape
    sizes = (ATT_Q_W, ATT_KV_W, ATT_KV_W, ML_QK_W, ML_QK_W, ML_V_W, ML_V_W, N_GATES)
    offs = np.concatenate([[0], np.cumsum(sizes)])
    q, k, v, mq, mk, mv, mo, g = [w[:, :, offs[i]:offs[i + 1]] for i in range(8)]
    main = jnp.concatenate([q, mv, mo, mq, mk, k, v], axis=2).astype(BF16)
    g_pad = jnp.concatenate([g, jnp.zeros((nl, d, LANES - N_GATES), w.dtype)], axis=2).astype(BF16)
    return main, g_pad, jnp.swapaxes(g, 1, 2).astype(BF16)


def kernel(x, c, ctx, c_ctx, w_ada, b_ada, norm1_w, norm2_w, w_in, b_gates, q_norm_w, k_norm_w, attn_sink,
           mlstm_norm_w, w_out, w_router, b_router, w_gate_up, b_gate_up, w_down, b_down):
    B, S, D = x.shape
    C = ctx.shape[1]
    depth = w_ada.shape[0]
    E = w_router.shape[-1]
    n_lat_rows = B * S
    R = n_lat_rows + B * C
    assert B < 16 and S % 1024 == 0 and C % 256 == 0 and (B * C) % 1024 == 0 and D % 128 == 0

    xall = jnp.concatenate([x.reshape(n_lat_rows, D), ctx.reshape(B * C, D)], axis=0)
    cv = jnp.zeros((16, D), F32).at[:B].set(c).at[B].set(c_ctx)
    mod = _ada_mod(cv, w_ada, b_ada)

    rope_tm = 512
    tabs = _rope_tables(S, rope_tm)
    reps = ATT_KV_W // ATT_HEAD_DIM
    gi = np.arange(ATT_KV_W) // ATT_HEAD_DIM
    bd = jnp.asarray(gi[:, None] == gi[None, :], BF16)

    w_main, w_g, w_gt = _permute_w_in(w_in)
    w_out_bf = w_out.astype(BF16)

    for l in range(depth):
        last = l == depth - 1
        mod3 = mod[l].reshape(16 * 6, 1, D)
        proj, g, gt = _inproj(xall, mod3, norm1_w[l], w_main, w_g, w_gt, l, n_lat_rows=n_lat_rows, seq=S, nbatch=B)
        q, kt = _qkrope(proj, tabs, jnp.tile(q_norm_w[l], reps).reshape(1, -1),
                        jnp.tile(k_norm_w[l], reps).reshape(1, -1), bd, n_lat_rows=n_lat_rows, seq=S)
        att = _attention(q, kt, proj, attn_sink[l], nbatch=B, seq=S, ctx_len=C, with_ctx_queries=not last)
        hf, hb = _mlstm(proj, g, gt, b_gates[l], nbatch=B, seq=S, ctx_len=C)

        T = n_lat_rows if last else R
        wr_t = w_router[l].T
        wr_hi = wr_t.astype(BF16)
        wr_lo = (wr_t - wr_hi.astype(F32)).astype(BF16)
        xmid, fx, lt = _outproj(att, hf, hb, proj, xall, mod3, mlstm_norm_w[l], norm2_w[l], w_out_bf,
                                wr_hi, wr_lo, b_router[l].reshape(E, 1), l,
                                n_rows=T, n_lat_rows=n_lat_rows, seq=S, nbatch=B)
        wts, pos, meta, code_init = _route(lt, tm=MOE_TM)
        code = _invert(pos.reshape(-1), code_init.reshape(-1))
        hs = _expert_up(meta, code, fx, w_gate_up, b_gate_up, l)
        yk = _expert_down(meta, code, hs, w_down, b_down, l, n_tok=T)
        xall = _combine(yk, xmid, wts.T, mod3, n_lat_rows=n_lat_rows, seq=S, nbatch=B)

    return xall[:n_lat_rows].reshape(B, S, D)
```

```python
import functools

import numpy as np
import jax
import jax.numpy as jnp
from jax import lax
from jax.experimental import pallas as pl
from jax.experimental.pallas import tpu as pltpu

F32 = jnp.float32
BF16 = jnp.bfloat16

GRID_W = 64
EPS = 1e-6
NEG_INF = -1e30
ATT_HEADS = 16
ATT_KV_HEADS = 4
ATT_HEAD_DIM = 64
ATT_GROUP = ATT_HEADS // ATT_KV_HEADS
WINDOW = 128
ATT_BLOCK = 128
ROPE_THETA = 10000.0
ML_HEADS = 4
ML_DQK = 128
ML_DV = 256
ML_CHUNK = 128
GATE_SOFTCAP = 15.0
TOP_K = 4
SWIGLU_LIMIT = 7.0
SWIGLU_ALPHA = 1.702

ATT_Q_W = ATT_HEADS * ATT_HEAD_DIM
ATT_KV_W = ATT_KV_HEADS * ATT_HEAD_DIM
ML_QK_W = ML_HEADS * ML_DQK
ML_V_W = ML_HEADS * ML_DV
N_GATES = 4 * ML_HEADS

COL_Q = 0
COL_MV = COL_Q + ATT_Q_W
COL_MO = COL_MV + ML_V_W
COL_MQ = COL_MO + ML_V_W
COL_MK = COL_MQ + ML_QK_W
COL_K = COL_MK + ML_QK_W
COL_V = COL_K + ATT_KV_W
PROJ_W = COL_V + ATT_KV_W
LOG2E = 1.4426950408889634
Q_SCALE = ATT_HEAD_DIM ** -0.5 * LOG2E

LANES = 128
VMEM_LIMIT = 56 * 1024 * 1024

MOE_TM = 256
TOK_TM = 256


def _dot(a, b):
    return jnp.dot(a, b, preferred_element_type=F32)


def _dot_nt(a, b):
    return lax.dot_general(a, b, (((1,), (1,)), ((), ())), preferred_element_type=F32)


def _split2(a):
    hi = a.astype(BF16)
    lo = (a - hi.astype(F32)).astype(BF16)
    return hi, lo


def _cparams(sem):
    return pltpu.CompilerParams(dimension_semantics=sem, vmem_limit_bytes=VMEM_LIMIT)


def _ada_kernel(cv_ref, w_ref, b_ref, o_ref):
    cv = cv_ref[...]
    s = cv * jax.nn.sigmoid(cv)
    o_ref[0] = _dot(s.astype(BF16), w_ref[0].astype(BF16)) + b_ref[0]


def _ada_mod(cv, w_ada, b_ada):
    L, D, N = w_ada.shape
    tn = 1024
    return pl.pallas_call(
        _ada_kernel,
        grid=(L, N // tn),
        in_specs=[pl.BlockSpec((16, D), lambda l, j: (0, 0)),
                  pl.BlockSpec((1, D, tn), lambda l, j: (l, 0, j)),
                  pl.BlockSpec((1, 1, tn), lambda l, j: (l, 0, j))],
        out_specs=pl.BlockSpec((1, 16, tn), lambda l, j: (l, 0, j)),
        out_shape=jax.ShapeDtypeStruct((L, 16, N), F32),
        compiler_params=_cparams(("arbitrary", "arbitrary")),
    )(cv, w_ada, b_ada.reshape(L, 1, N))


def _inproj_kernel(x_ref, sc_ref, sh_ref, nw_ref, w_ref, wg_ref, wgt_ref, p_ref, g_ref, gt_ref, hn_ref):
    @pl.when(pl.program_id(1) == 0)
    def _():
        x = x_ref[...]
        ms = jnp.mean(x * x, axis=-1, keepdims=True)
        y = x * lax.rsqrt(ms + EPS) * nw_ref[...]
        hb = (y * (1.0 + sc_ref[0]) + sh_ref[0]).astype(BF16)
        hn_ref[...] = hb
        g_ref[...] = _dot(hb, wg_ref[0])
        gt_ref[...] = _dot_nt(wgt_ref[0], hb)

    p_ref[...] = _dot(hn_ref[...], w_ref[0]).astype(BF16)


def _inproj(xall, mod3, norm_w, w_main, w_g, w_gt, layer, *, n_lat_rows, seq, nbatch):
    R, D = xall.shape
    tm, tn = 1024, 1536
    n_lat_tiles = n_lat_rows // tm
    per_b = seq // tm

    def mod_idx(which):
        return lambda i, j: (jnp.where(i < n_lat_tiles, i // per_b, nbatch) * 6 + which, 0, 0)

    return pl.pallas_call(
        _inproj_kernel,
        grid=(R // tm, PROJ_W // tn),
        in_specs=[pl.BlockSpec((tm, D), lambda i, j: (i, 0)),
                  pl.BlockSpec((1, 1, D), mod_idx(1)),
                  pl.BlockSpec((1, 1, D), mod_idx(0)),
                  pl.BlockSpec((1, D), lambda i, j: (0, 0)),
                  pl.BlockSpec((1, D, tn), lambda i, j: (layer, 0, j)),
                  pl.BlockSpec((1, D, LANES), lambda i, j: (layer, 0, 0)),
                  pl.BlockSpec((1, N_GATES, D), lambda i, j: (layer, 0, 0))],
        out_specs=[pl.BlockSpec((tm, tn), lambda i, j: (i, j)),
                   pl.BlockSpec((tm, LANES), lambda i, j: (i, 0)),
                   pl.BlockSpec((N_GATES, tm), lambda i, j: (0, i))],
        out_shape=[jax.ShapeDtypeStruct((R, PROJ_W), BF16),
                   jax.ShapeDtypeStruct((R, LANES), F32),
                   jax.ShapeDtypeStruct((N_GATES, R), F32)],
        scratch_shapes=[pltpu.VMEM((tm, D), BF16)],
        compiler_params=_cparams(("arbitrary", "arbitrary")),
    )(xall, mod3, mod3, norm_w.reshape(1, D), w_main, w_g, w_gt)


def _norm_rope(x, w_row, cos, sa, sb, bd, scale):
    x = x.astype(F32)
    hi, lo = _split2(x * x)
    ss = _dot(hi, bd) + _dot(lo, bd)
    y = x * lax.rsqrt(ss * (1.0 / ATT_HEAD_DIM) + EPS) * w_row
    w = y.shape[-1]
    r = y * cos + pltpu.roll(y, 16, 1) * sa + pltpu.roll(y, w - 16, 1) * sb
    if scale != 1.0:
        r = r * scale
    return r


def _qkrope_kernel(q_ref, k_ref, cos_ref, sa_ref, sb_ref, qw_ref, kw_ref, bd_ref, q_out, kt_out):
    cos, sa, sb, bd = cos_ref[...], sa_ref[...], sb_ref[...], bd_ref[...]
    cw = cos.shape[-1]
    for c in range(ATT_Q_W // cw):
        q_out[:, c * cw:(c + 1) * cw] = _norm_rope(q_ref[:, c * cw:(c + 1) * cw], qw_ref[...], cos, sa, sb, bd,
                                                   Q_SCALE).astype(BF16)
    kt_out[...] = _norm_rope(k_ref[...], kw_ref[...], cos, sa, sb, bd, 1.0).T.astype(BF16)


def _qkrope(proj, tabs, qw, kw, bd, *, n_lat_rows, seq):
    R = proj.shape[0]
    tm = 512
    n_lat_tiles = n_lat_rows // tm
    per_b = seq // tm
    cw = ATT_KV_W
    tab_spec = pl.BlockSpec((tm, cw), lambda i: (jnp.where(i < n_lat_tiles, i % per_b, per_b), 0))
    const = lambda i: (0, 0)
    return pl.pallas_call(
        _qkrope_kernel,
        grid=(R // tm,),
        in_specs=[pl.BlockSpec((tm, ATT_Q_W), lambda i: (i, 0)),
                  pl.BlockSpec((tm, ATT_KV_W), lambda i: (i, COL_K // ATT_KV_W)),
                  tab_spec, tab_spec, tab_spec,
                  pl.BlockSpec((1, cw), const), pl.BlockSpec((1, cw), const), pl.BlockSpec((cw, cw), const)],
        out_specs=[pl.BlockSpec((tm, ATT_Q_W), lambda i: (i, 0)),
                   pl.BlockSpec((ATT_KV_W, tm), lambda i: (0, i))],
        out_shape=[jax.ShapeDtypeStruct((R, ATT_Q_W), BF16), jax.ShapeDtypeStruct((ATT_KV_W, R), BF16)],
        compiler_params=_cparams(("arbitrary",)),
    )(proj, proj, tabs[0], tabs[1], tabs[2], qw, kw, bd)


def _rope_tables(seq, tm):
    rows = seq // GRID_W
    row = jnp.repeat(jnp.arange(rows, dtype=F32), GRID_W)
    col = jnp.tile(jnp.arange(GRID_W, dtype=F32), rows)
    half = ATT_HEAD_DIM // 2
    inv_freq = ROPE_THETA ** (-jnp.arange(0, half, 2, dtype=F32) / half)

    def ang(p):
        a = p[:, None] * inv_freq[None, :]
        return jnp.concatenate([a, a], axis=-1)

    a = jnp.concatenate([ang(row), ang(col)], axis=-1)
    cos, sin = jnp.cos(a), jnp.sin(a)
    second = (jnp.arange(ATT_HEAD_DIM) % half) >= (half // 2)
    sa = jnp.where(second[None, :], sin, 0.0)
    sb = jnp.where(second[None, :], 0.0, -sin)
    reps = ATT_KV_W // ATT_HEAD_DIM

    def fin(t, fill):
        t = jnp.tile(t, (1, reps))
        return jnp.concatenate([t, jnp.full((tm, t.shape[1]), fill, F32)], axis=0)

    return fin(cos, 1.0), fin(sa, 0.0), fin(sb, 0.0)


def _attn_kernel(sink_ref, q_ref, kp_ref, kc_ref, kn_ref, vp_ref, vc_ref, vn_ref, kx_ref, vx_ref, o_ref, *, n_lat):
    n = pl.program_id(1)
    blk, hd = ATT_BLOCK, ATT_HEAD_DIM
    n_loc = 3 * blk
    kt = jnp.concatenate([kp_ref[...], kc_ref[...], kn_ref[...], kx_ref[...]], axis=1)
    v = jnp.concatenate([vp_ref[...], vc_ref[...], vn_ref[...], vx_ref[...]], axis=0)
    row = lax.broadcasted_iota(jnp.int32, (2 * blk, n_loc), 0)
    r = jnp.where(row >= blk, row - blk, row)
    w = lax.broadcasted_iota(jnp.int32, (2 * blk, n_loc), 1)
    rel = w - r
    jabs = n * blk - WINDOW + w
    ok = (rel >= 0) & (rel <= 2 * WINDOW) & (jabs >= 0) & (jabs < n_lat * blk) & (n < n_lat)
    lo_half = lax.broadcasted_iota(jnp.int32, (1, 2 * hd), 1) < hd
    is_a = lax.broadcasted_iota(jnp.int32, (2 * blk, 1), 0) < blk
    zero = jnp.zeros((), BF16)
    one = jnp.ones((), BF16)
    for h in range(ATT_KV_HEADS):
        kth = kt[h * hd:(h + 1) * hd, :]
        rhs_s = jnp.concatenate([kth, kth], axis=0)
        vt = v[:, (h // 2) * 2 * hd:(h // 2 + 1) * 2 * hd]
        v_lo = vt if h % 2 == 0 else jnp.concatenate([vt[:, hd:], vt[:, :hd]], axis=1)
        rhs_v = jnp.where(lo_half, v_lo, one)
        for pr in range(ATT_GROUP // 2):
            ha = h * ATT_GROUP + 2 * pr
            qp = q_ref[:, ha * hd:(ha + 2) * hd]
            lhs = jnp.concatenate([jnp.where(lo_half, qp, zero), jnp.where(lo_half, zero, qp)], axis=0)
            s = _dot(lhs, rhs_s)
            s_loc = jnp.where(ok, s[:, :n_loc], NEG_INF)
            s_ctx = s[:, n_loc:]
            sink = jnp.where(is_a, sink_ref[ha], sink_ref[ha + 1]) * LOG2E
            m = jnp.maximum(jnp.maximum(jnp.max(s_loc, axis=-1, keepdims=True),
                                        jnp.max(s_ctx, axis=-1, keepdims=True)), sink)
            p = jnp.concatenate([jnp.exp2(s_loc - m), jnp.exp2(s_ctx - m)], axis=1).astype(BF16)
            res = _dot(p, rhs_v)
            e_sink = jnp.exp2(sink - m)
            ra, rb = res[:blk], res[blk:]
            num = jnp.where(lo_half, ra, pltpu.roll(rb, hd, 1))
            den = (jnp.where(lo_half, pltpu.roll(ra, hd, 1), rb)
                   + jnp.where(lo_half, e_sink[:blk], e_sink[blk:]))
            o_ref[:, ha * hd:(ha + 2) * hd] = (num / den).astype(BF16)


def _attention(q, kt, proj, sink, *, nbatch, seq, ctx_len, with_ctx_queries):
    blk = ATT_BLOCK
    n_lat = seq // blk
    n_ctx = ctx_len // blk
    nb = n_lat + (n_ctx if with_ctx_queries else 0)
    lat_blocks = nbatch * n_lat
    out_rows = nbatch * nb * blk
    vcol = COL_V // ATT_KV_W

    def qrow(b, n):
        return jnp.where(n < n_lat, b * n_lat + n, lat_blocks + b * n_ctx + (n - n_lat))

    def win(off):
        return lambda b, n: b * n_lat + jnp.clip(n + off, 0, n_lat - 1)

    def spec_k(off):
        f = win(off)
        return pl.BlockSpec((ATT_KV_W, blk), lambda b, n: (0, f(b, n)))

    def spec_v(off):
        f = win(off)
        return pl.BlockSpec((blk, ATT_KV_W), lambda b, n: (f(b, n), vcol))

    ctx_row = lambda b: (nbatch * seq) // ctx_len + b
    return pl.pallas_call(
        functools.partial(_attn_kernel, n_lat=n_lat),
        grid=(nbatch, nb),
        in_specs=[pl.BlockSpec(memory_space=pltpu.SMEM),
                  pl.BlockSpec((blk, ATT_Q_W), lambda b, n: (qrow(b, n), 0)),
                  spec_k(-1), spec_k(0), spec_k(1), spec_v(-1), spec_v(0), spec_v(1),
                  pl.BlockSpec((ATT_KV_W, ctx_len), lambda b, n: (0, ctx_row(b))),
                  pl.BlockSpec((ctx_len, ATT_KV_W), lambda b, n: (ctx_row(b), vcol))],
        out_specs=pl.BlockSpec((blk, ATT_Q_W), lambda b, n: (qrow(b, n), 0)),
        out_shape=jax.ShapeDtypeStruct((out_rows, ATT_Q_W), BF16),
        compiler_params=_cparams(("arbitrary", "arbitrary")),
    )(sink, q, kt, kt, kt, proj, proj, proj, kt, proj)


def _softcap(g):
    return GATE_SOFTCAP * jnp.tanh(g * (1.0 / GATE_SOFTCAP))


def _log_sigmoid(x):
    return jnp.minimum(x, 0.0) - jnp.log(1.0 + jnp.exp(-jnp.abs(x)))


def _mlstm_kernel(bg_ref, bgt_ref, gf_ref, gtf_ref, gb_ref, gtb_ref, qf_ref, kf_ref, vf_ref, qb_ref, kb_ref, vb_ref,
                  hf_ref, hb_ref, c_ref, n_ref, m_ref):
    L = ML_CHUNK

    @pl.when(pl.program_id(1) == 0)
    def _():
        c_ref[...] = jnp.zeros_like(c_ref)
        n_ref[...] = jnp.zeros_like(n_ref)
        m_ref[...] = jnp.zeros_like(m_ref)

    ri = lax.broadcasted_iota(jnp.int32, (L, L), 0)
    ci = lax.broadcasted_iota(jnp.int32, (L, L), 1)
    lower = ci <= ri
    upper = ci >= ri
    lower_b = lower.astype(BF16)
    upper_b = upper.astype(BF16)
    scale = ML_DQK ** -0.5

    dirs = ((gf_ref, gtf_ref, qf_ref, kf_ref, vf_ref, hf_ref, lower, lower_b, upper_b, L - 1),
            (gb_ref, gtb_ref, qb_ref, kb_ref, vb_ref, hb_ref, upper, upper_b, lower_b, 0))
    for d, (g_ref, gt_ref, q_ref, k_ref, v_ref, h_ref, tri, tri_b, trit_b, last) in enumerate(dirs):
        g = _softcap(g_ref[...] + bg_ref[...])
        gt = _softcap(gt_ref[...] + bgt_ref[...])
        lf_hi, lf_lo = _split2(_log_sigmoid(g))
        lft_hi, lft_lo = _split2(_log_sigmoid(gt))
        bcol_all = _dot(tri_b, lf_hi) + _dot(tri_b, lf_lo)
        brow_all = _dot(lft_hi, trit_b) + _dot(lft_lo, trit_b)
        for h in range(ML_HEADS):
            r = d * ML_HEADS + h
            ic = d * 2 * ML_HEADS + h
            fc = ic + ML_HEADS
            i_col, b_col = g[:, ic:ic + 1], bcol_all[:, fc:fc + 1]
            i_row, b_row = gt[ic:ic + 1, :], brow_all[fc:fc + 1, :]
            total = b_col[last:last + 1, :]
            m_prev = m_ref[r:r + 1, 0:1]
            inter = b_col + m_prev
            dmat = jnp.where(tri, b_col - b_row + i_row, NEG_INF)
            m_t = jnp.maximum(inter, jnp.max(dmat, axis=-1, keepdims=True))
            q = q_ref[:, h * ML_DQK:(h + 1) * ML_DQK]
            k_s = k_ref[:, h * ML_DQK:(h + 1) * ML_DQK].astype(F32) * scale
            v = v_ref[:, h * ML_DV:(h + 1) * ML_DV]
            w_intra = _dot_nt(q, k_s.astype(BF16)) * jnp.exp(dmat - m_t)
            w_inter = jnp.exp(inter - m_t)
            c_t = c_ref[r]
            n_row = n_ref[r:r + 1, :]
            num = _dot(w_intra.astype(BF16), v) + w_inter * _dot(q, c_t.astype(BF16))
            qn = jnp.sum(q.astype(F32) * n_row, axis=-1, keepdims=True)
            den = jnp.sum(w_intra, axis=-1, keepdims=True) + w_inter * qn
            h_out = num / jnp.maximum(jnp.abs(den), jnp.exp(-m_t))
            h_ref[:, h * ML_DV:(h + 1) * ML_DV] = h_out.astype(BF16)
            g_row = total - b_row + i_row
            g_col = total - b_col + i_col
            m_new = jnp.maximum(total + m_prev, jnp.max(g_row, axis=-1, keepdims=True))
            ws_col = jnp.exp(g_col - m_new)
            w_c = jnp.exp(total + m_prev - m_new)
            c_ref[r] = w_c * c_t + _dot(k_s.T.astype(BF16), (ws_col * v.astype(F32)).astype(BF16))
            n_ref[r:r + 1, :] = w_c * n_row + jnp.sum(ws_col * k_s, axis=0, keepdims=True)
            m_ref[r:r + 1, :] = jnp.broadcast_to(m_new, (1, LANES))


def _mlstm(proj, g, gt, b_gates, *, nbatch, seq, ctx_len):
    L = ML_CHUNK
    R = proj.shape[0]
    n_lat = seq // L
    n_ctx = ctx_len // L
    steps = n_ctx + n_lat
    lat_blocks = nbatch * n_lat

    def fwd(b, t):
        return jnp.where(t < n_ctx, lat_blocks + b * n_ctx + t, b * n_lat + t - n_ctx)

    def bwd(b, t):
        return jnp.where(t < n_ctx, lat_blocks + b * n_ctx + (n_ctx - 1 - t), b * n_lat + (steps - 1 - t))

    def rows(f, width, colblk):
        return pl.BlockSpec((L, width), lambda b, t: (f(b, t), colblk))

    def gt_spec(f):
        return pl.BlockSpec((N_GATES, L), lambda b, t: (0, f(b, t)))

    bg = jnp.zeros((1, LANES), F32).at[0, :N_GATES].set(b_gates)
    bgt = b_gates.reshape(N_GATES, 1)
    const = lambda b, t: (0, 0)
    qc, kc, vc = COL_MQ // ML_QK_W, COL_MK // ML_QK_W, COL_MV // ML_V_W
    return pl.pallas_call(
        _mlstm_kernel,
        grid=(nbatch, steps),
        in_specs=[pl.BlockSpec((1, LANES), const), pl.BlockSpec((N_GATES, 1), const),
                  rows(fwd, LANES, 0), gt_spec(fwd), rows(bwd, LANES, 0), gt_spec(bwd),
                  rows(fwd, ML_QK_W, qc), rows(fwd, ML_QK_W, kc), rows(fwd, ML_V_W, vc),
                  rows(bwd, ML_QK_W, qc), rows(bwd, ML_QK_W, kc), rows(bwd, ML_V_W, vc)],
        out_specs=[rows(fwd, ML_V_W, 0), rows(bwd, ML_V_W, 0)],
        out_shape=[jax.ShapeDtypeStruct((R, ML_V_W), BF16), jax.ShapeDtypeStruct((R, ML_V_W), BF16)],
        scratch_shapes=[pltpu.VMEM((2 * ML_HEADS, ML_DQK, ML_DV), F32),
                        pltpu.VMEM((2 * ML_HEADS, ML_DQK), F32),
                        pltpu.VMEM((2 * ML_HEADS, LANES), F32)],
        compiler_params=_cparams(("arbitrary", "arbitrary")),
    )(bg, bgt, g, gt, g, gt, proj, proj, proj, proj, proj, proj)


def _outproj_kernel(att_ref, hf_ref, hb_ref, mo_ref, x_ref, g1_ref, sc2_ref, sh2_ref, mnw_ref, n2w_ref,
                    wo_ref, wrh_ref, wrl_ref, br_ref, xo_ref, fx_ref, lt_ref):
    hsum = hf_ref[...].astype(F32) + hb_ref[...].astype(F32)
    parts = []
    for h in range(ML_HEADS):
        hh = hsum[:, h * ML_DV:(h + 1) * ML_DV]
        ms = jnp.mean(hh * hh, axis=-1, keepdims=True)
        parts.append(hh * lax.rsqrt(ms + EPS))
    ml = jnp.concatenate(parts, axis=1) * mnw_ref[...] * jax.nn.sigmoid(mo_ref[...].astype(F32))
    y = _dot(att_ref[...], wo_ref[0, :ATT_Q_W, :]) + _dot(ml.astype(BF16), wo_ref[0, ATT_Q_W:, :])
    x = x_ref[...] + g1_ref[0] * y
    xo_ref[...] = x
    ms = jnp.mean(x * x, axis=-1, keepdims=True)
    fx = (x * lax.rsqrt(ms + EPS) * n2w_ref[...]) * (1.0 + sc2_ref[0]) + sh2_ref[0]
    nc = fx.shape[1] // LANES
    for c in range(nc):
        fx_ref[pl.ds(c, fx.shape[0], stride=nc), :] = fx[:, c * LANES:(c + 1) * LANES]
    f_hi, f_lo = _split2(fx)
    wrh = wrh_ref[...]
    lt_ref[...] = _dot_nt(wrh, f_hi) + _dot_nt(wrh, f_lo) + _dot_nt(wrl_ref[...], f_hi) + br_ref[...]


def _outproj(att, hf, hb, proj, xall, mod3, mnw, n2w, wo, wr_hi, wr_lo, br, layer, *, n_rows, n_lat_rows, seq,
             nbatch):
    D = xall.shape[1]
    E = wr_hi.shape[0]
    tm = TOK_TM
    n_lat_tiles = n_lat_rows // tm
    per_b = seq // tm

    def mod_idx(which):
        return lambda i: (jnp.where(i < n_lat_tiles, i // per_b, nbatch) * 6 + which, 0, 0)

    row = lambda i: (i, 0)
    const = lambda i: (0, 0)
    return pl.pallas_call(
        _outproj_kernel,
        grid=(n_rows // tm,),
        in_specs=[pl.BlockSpec((tm, ATT_Q_W), row), pl.BlockSpec((tm, ML_V_W), row), pl.BlockSpec((tm, ML_V_W), row),
                  pl.BlockSpec((tm, ML_V_W), lambda i: (i, COL_MO // ML_V_W)),
                  pl.BlockSpec((tm, D), row),
                  pl.BlockSpec((1, 1, D), mod_idx(2)), pl.BlockSpec((1, 1, D), mod_idx(4)),
                  pl.BlockSpec((1, 1, D), mod_idx(3)),
                  pl.BlockSpec((1, ML_V_W), const), pl.BlockSpec((1, D), const),
                  pl.BlockSpec((1, ATT_Q_W + ML_V_W, D), lambda i: (layer, 0, 0)),
                  pl.BlockSpec((E, D), const), pl.BlockSpec((E, D), const), pl.BlockSpec((E, 1), const)],
        out_specs=[pl.BlockSpec((tm, D), row), pl.BlockSpec((tm * (D // LANES), LANES), row),
                   pl.BlockSpec((E, tm), lambda i: (0, i))],
        out_shape=[jax.ShapeDtypeStruct((n_rows, D), F32), jax.ShapeDtypeStruct((n_rows * (D // LANES), LANES), F32),
                   jax.ShapeDtypeStruct((E, n_rows), F32)],
        compiler_params=_cparams(("arbitrary",)),
    )(att, hf, hb, proj, xall, mod3, mod3, mod3, mnw.reshape(1, ML_V_W), n2w.reshape(1, D), wo, wr_hi, wr_lo, br)


def _route_kernel(lt_ref, wts_ref, pos_ref, meta_ref, init_ref, idx_ref, *, chunk, sub, tm):
    E, T = lt_ref.shape
    eio = lax.broadcasted_iota(jnp.int32, (E, chunk), 0)
    ui = lax.broadcasted_iota(jnp.int32, (sub, sub), 0)
    uj = lax.broadcasted_iota(jnp.int32, (sub, sub), 1)
    strict_upper = (ui < uj).astype(BF16)

    carry = jnp.zeros((E, 1), F32)
    for c in range(T // chunk):
        sl = slice(c * chunk, (c + 1) * chunk)
        cur = lt_ref[:, sl]
        member = jnp.zeros((E, chunk), F32)
        vals, hots = [], []
        for k in range(TOP_K):
            mx = jnp.max(cur, axis=0, keepdims=True)
            idx = jnp.min(jnp.where(cur == mx, eio, E), axis=0, keepdims=True)
            hot = eio == idx
            cur = jnp.where(hot, -jnp.inf, cur)
            member = member + hot.astype(F32)
            idx_ref[k:k + 1, sl] = idx
            vals.append(mx)
            hots.append(hot)
        ex = [jnp.exp(v - vals[0]) for v in vals]
        tot = ex[0] + ex[1] + ex[2] + ex[3]
        for k in range(TOP_K):
            wts_ref[k:k + 1, sl] = ex[k] / tot
        ranks = []
        for s in range(chunk // sub):
            mb = member[:, s * sub:(s + 1) * sub]
            ranks.append(_dot(mb.astype(BF16), strict_upper) + carry)
            carry = carry + jnp.sum(mb, axis=1, keepdims=True)
        rank = jnp.concatenate(ranks, axis=1)
        for k in range(TOP_K):
            pos_ref[k:k + 1, sl] = jnp.sum(jnp.where(hots[k], rank, 0.0), axis=0, keepdims=True).astype(jnp.int32)

    ei = lax.broadcasted_iota(jnp.int32, (E, E), 0)
    ej = lax.broadcasted_iota(jnp.int32, (E, E), 1)

    def to_row(col):
        return jnp.sum(jnp.where(ei == ej, col, 0.0), axis=0, keepdims=True)

    def excl_cumsum_col(row):
        return jnp.sum(jnp.where(ej < ei, row, 0.0), axis=1, keepdims=True)

    cnt = carry
    inv = 1.0 / tm
    padded = jnp.floor((cnt + (tm - 1.0)) * inv) * tm
    start = excl_cumsum_col(to_row(padded))
    end = start + padded
    n_used = jnp.sum(padded, axis=0, keepdims=True) * inv

    for c in range(T // chunk):
        sl = slice(c * chunk, (c + 1) * chunk)
        for k in range(TOP_K):
            off = jnp.sum(jnp.where(eio == idx_ref[k:k + 1, sl], start, 0.0), axis=0, keepdims=True)
            pos_ref[k:k + 1, sl] = pos_ref[k:k + 1, sl] + off.astype(jnp.int32) + tm

    nw = meta_ref.shape[1]
    wi = lax.broadcasted_iota(jnp.int32, (E, nw), 1).astype(F32)
    wv = jnp.minimum(wi, n_used - 1.0)
    te = jnp.sum((end <= wv * tm).astype(F32), axis=0, keepdims=True)
    nu = jnp.broadcast_to(n_used, (1, nw))
    zeros = jnp.zeros((1, nw), F32)
    meta_ref[...] = jnp.concatenate([te, nu, zeros, zeros, zeros, zeros, zeros, zeros], axis=0).astype(jnp.int32)

    ns = init_ref.shape[0]
    tile = lax.broadcasted_iota(jnp.int32, (ns, 1), 0).astype(F32) - 1.0
    tv = jnp.clip(tile, 0.0, n_used - 1.0)
    te_col = jnp.sum((to_row(end) <= tv * tm).astype(F32), axis=1, keepdims=True)
    trash_e = jnp.where((tile < 0.0) | (tile >= n_used), float(E), te_col)
    lane = lax.broadcasted_iota(jnp.int32, (ns, tm), 1).astype(F32)
    init_ref[...] = (float(TOP_K * T) + trash_e * tm + lane).astype(jnp.int32)


def _route(lt, *, tm):
    E, T = lt.shape
    nt = T * TOP_K // tm + E
    nw_pad = -(-(nt + 1) // LANES) * LANES
    return pl.pallas_call(
        functools.partial(_route_kernel, chunk=2048 if T % 2048 == 0 else 1024, sub=512, tm=tm),
        out_shape=[jax.ShapeDtypeStruct((TOP_K, T), F32), jax.ShapeDtypeStruct((TOP_K, T), jnp.int32),
                   jax.ShapeDtypeStruct((8, nw_pad), jnp.int32), jax.ShapeDtypeStruct((nt + 2, tm), jnp.int32)],
        scratch_shapes=[pltpu.VMEM((TOP_K, T), jnp.int32)],
        compiler_params=pltpu.CompilerParams(vmem_limit_bytes=VMEM_LIMIT),
    )(lt)


def _invert_kernel(pos_ref, init_ref, code_ref, sem, *, n_tok):
    cp = pltpu.make_async_copy(init_ref, code_ref, sem)
    cp.start()
    cp.wait()

    def body(t, c):
        for k in range(TOP_K):
            code_ref[pos_ref[k * n_tok + t]] = k * n_tok + t
        return c

    lax.fori_loop(0, n_tok, body, 0, unroll=4)


def _invert(pos_flat, init_flat):
    n_tok = pos_flat.shape[0] // TOP_K
    return pl.pallas_call(
        functools.partial(_invert_kernel, n_tok=n_tok),
        grid_spec=pltpu.PrefetchScalarGridSpec(
            num_scalar_prefetch=1,
            grid=(1,),
            in_specs=[pl.BlockSpec(memory_space=pl.ANY)],
            out_specs=pl.BlockSpec(memory_space=pltpu.SMEM),
            scratch_shapes=[pltpu.SemaphoreType.DMA(())]),
        out_shape=jax.ShapeDtypeStruct(init_flat.shape, jnp.int32),
        compiler_params=_cparams(("arbitrary",)),
    )(pos_flat, init_flat)


def _cast_expert_weight(w_ref, wbf_ref):
    rows_per = min(256, wbf_ref.shape[0])

    def cast(j, c):
        sl = pl.ds(pl.multiple_of(j * rows_per, rows_per), rows_per)
        wbf_ref[sl, :] = w_ref[0, 0, sl, :].astype(BF16)
        return c

    lax.fori_loop(0, wbf_ref.shape[0] // rows_per, cast, 0)


def _up_kernel(te_ref, nu_ref, code_ref, fx_ref, w_ref, b_ref, h_ref, xa_ref, xb_ref, wbf_ref, sem, *, n_tok):
    i = pl.program_id(0)
    n_used = nu_ref[0]
    nc = w_ref.shape[2] // LANES
    tm = xa_ref.shape[0] // nc
    bufs = (xa_ref, xb_ref)

    def gather(tile, par):
        base = (tile + 1) * tm
        for r in range(tm):
            c = code_ref[base + r]
            k = sum((c >= j * n_tok).astype(jnp.int32) for j in range(1, TOP_K))
            t = jnp.minimum(c - k * n_tok, n_tok - 1)
            pltpu.make_async_copy(fx_ref.at[pl.ds(pl.multiple_of(t * nc, nc), nc)], bufs[par].at[pl.ds(r * nc, nc)],
                                  sem.at[par]).start(priority=r % 2)

    def wait(par):
        pltpu.make_async_copy(fx_ref.at[pl.ds(0, tm * nc)], bufs[par], sem.at[par]).wait()

    def rows(par):
        return jnp.concatenate([bufs[par][pl.ds(c, tm, stride=nc), :] for c in range(nc)], axis=1)

    @pl.when(i == 0)
    def _():
        gather(0, 0)

    @pl.when((i < n_used) & ((i == 0) | (te_ref[i] != te_ref[jnp.maximum(i - 1, 0)])))
    def _():
        _cast_expert_weight(w_ref, wbf_ref)

    for par in range(2):
        @pl.when((i < n_used) & (i % 2 == par))
        def _():
            wait(par)
            gather(i + 1, 1 - par)
            y = _dot(rows(par).astype(BF16), wbf_ref[...]) + b_ref[0, 0]
            f = y.shape[1] // 2
            gate = jnp.minimum(y[:, :f], SWIGLU_LIMIT)
            up = jnp.clip(y[:, f:], -SWIGLU_LIMIT, SWIGLU_LIMIT)
            h_ref[...] = ((up + 1.0) * gate * jax.nn.sigmoid(SWIGLU_ALPHA * gate)).astype(h_ref.dtype)

        @pl.when((i == n_used) & (i % 2 == par))
        def _():
            wait(par)

    @pl.when((i >= n_used) & (i < pl.num_programs(0) - 1))
    def _():
        h_ref[...] = jnp.zeros_like(h_ref)


def _expert_up(meta, code, fx, w_all, b_all, layer):
    _, E, D, N = w_all.shape
    nc = D // LANES
    T = fx.shape[0] // nc
    tm = MOE_TM
    nt = T * TOP_K // tm + E
    widx = lambda i, te, nu, code: (layer, te[i], 0, 0)
    return pl.pallas_call(
        functools.partial(_up_kernel, n_tok=T),
        grid_spec=pltpu.PrefetchScalarGridSpec(
            num_scalar_prefetch=3,
            grid=(nt + 1,),
            in_specs=[pl.BlockSpec(memory_space=pl.ANY),
                      pl.BlockSpec((1, 1, D, N), widx),
                      pl.BlockSpec((1, 1, 1, N), widx)],
            out_specs=pl.BlockSpec((tm, N // 2), lambda i, te, nu, code: (jnp.minimum(i, nt - 1), 0)),
            scratch_shapes=[pltpu.VMEM((tm * nc, LANES), F32), pltpu.VMEM((tm * nc, LANES), F32),
                            pltpu.VMEM((D, N), BF16), pltpu.SemaphoreType.DMA((2,))]),
        out_shape=jax.ShapeDtypeStruct((nt * tm, N // 2), BF16),
        compiler_params=_cparams(("arbitrary",)),
    )(meta[0], meta[1], code, fx, w_all, b_all.reshape(b_all.shape[0], E, 1, N))


def _down_kernel(te_ref, nu_ref, code_ref, h_ref, w_ref, b_ref, yk_ref, ya_ref, yb_ref, wbf_ref, sem, *, n_tok):
    i = pl.program_id(0)
    n_used = nu_ref[0]
    tm = ya_ref.shape[0]
    bufs = (ya_ref, yb_ref)

    def scatter(tile, par):
        base = (tile + 1) * tm
        for r in range(tm):
            dst = code_ref[base + r]
            pltpu.make_async_copy(bufs[par].at[pl.ds(r, 1)], yk_ref.at[pl.ds(dst, 1)], sem.at[par]).start(
                priority=r % 2)

    def wait(par):
        pltpu.make_async_copy(bufs[par], yk_ref.at[pl.ds(0, tm)], sem.at[par]).wait()

    @pl.when(i == 0)
    def _():
        yb_ref[...] = jnp.zeros_like(yb_ref)
        n_trash = (yk_ref.shape[0] - TOP_K * n_tok) // tm
        fills = [pltpu.make_async_copy(yb_ref, yk_ref.at[pl.ds(TOP_K * n_tok + j * tm, tm)], sem.at[1])
                 for j in range(n_trash)]
        for cp in fills:
            cp.start()
        for cp in fills:
            cp.wait()

    @pl.when((i < n_used) & ((i == 0) | (te_ref[i] != te_ref[jnp.maximum(i - 1, 0)])))
    def _():
        _cast_expert_weight(w_ref, wbf_ref)

    for par in range(2):
        @pl.when((i < n_used) & (i % 2 == par))
        def _():
            @pl.when(i >= 1)
            def _():
                wait(par)
            scatter(i - 1, 1 - par)
            bufs[par][...] = _dot(h_ref[...], wbf_ref[...]) + b_ref[0, 0]

        @pl.when((i == n_used) & (i % 2 == par))
        def _():
            scatter(i - 1, 1 - par)
            wait(par)
            wait(1 - par)


def _expert_down(meta, code, hs, w_all, b_all, layer, *, n_tok):
    _, E, F, D = w_all.shape
    tm = MOE_TM
    nt = hs.shape[0] // tm
    widx = lambda i, te, nu, code: (layer, te[i], 0, 0)
    return pl.pallas_call(
        functools.partial(_down_kernel, n_tok=n_tok),
        grid_spec=pltpu.PrefetchScalarGridSpec(
            num_scalar_prefetch=3,
            grid=(nt + 1,),
            in_specs=[pl.BlockSpec((tm, F), lambda i, te, nu, code: (jnp.minimum(i, nu[0] - 1), 0)),
                      pl.BlockSpec((1, 1, F, D), widx),
                      pl.BlockSpec((1, 1, 1, D), widx)],
            out_specs=pl.BlockSpec(memory_space=pl.ANY),
            scratch_shapes=[pltpu.VMEM((tm, D), F32), pltpu.VMEM((tm, D), F32), pltpu.VMEM((F, D), BF16),
                            pltpu.SemaphoreType.DMA((2,))]),
        out_shape=jax.ShapeDtypeStruct((TOP_K * n_tok + (E + 1) * tm, D), F32),
        compiler_params=_cparams(("arbitrary",)),
    )(meta[0], meta[1], code, hs, w_all, b_all.reshape(b_all.shape[0], E, 1, D))


def _combine_kernel(y0_ref, y1_ref, y2_ref, y3_ref, x_ref, w_ref, g2_ref, o_ref):
    wt = w_ref[...]
    y = wt[:, 0:1] * y0_ref[...]
    for k, y_ref in enumerate((y1_ref, y2_ref, y3_ref), start=1):
        y = y + wt[:, k:k + 1] * y_ref[...]
    o_ref[...] = x_ref[...] + g2_ref[0] * y


def _combine(yk, x, wts_t, mod3, *, n_lat_rows, seq, nbatch):
    T, D = x.shape
    tm = TOK_TM
    n_lat_tiles = n_lat_rows // tm
    per_b = seq // tm
    stride = T // tm
    planes = [pl.BlockSpec((tm, D), functools.partial(lambda i, k: (k * stride + i, 0), k=k)) for k in range(TOP_K)]
    return pl.pallas_call(
        _combine_kernel,
        grid=(T // tm,),
        in_specs=planes + [pl.BlockSpec((tm, D), lambda i: (i, 0)),
                           pl.BlockSpec((tm, TOP_K), lambda i: (i, 0)),
                           pl.BlockSpec((1, 1, D),
                                        lambda i: (jnp.where(i < n_lat_tiles, i // per_b, nbatch) * 6 + 5, 0, 0))],
        out_specs=pl.BlockSpec((tm, D), lambda i: (i, 0)),
        out_shape=jax.ShapeDtypeStruct((T, D), F32),
        compiler_params=_cparams(("arbitrary",)),
    )(yk, yk, yk, yk, x, wts_t, mod3)


def _permute_w_in(w):
    nl, d, _ = w.shape
    sizes = (ATT_Q_W, ATT_KV_W, ATT_KV_W, ML_QK_W, ML_QK_W, ML_V_W, ML_V_W, N_GATES)
    offs = np.concatenate([[0], np.cumsum(sizes)])
    q, k, v, mq, mk, mv, mo, g = [w[:, :, offs[i]:offs[i + 1]] for i in range(8)]
    main = jnp.concatenate([q, mv, mo, mq, mk, k, v], axis=2).astype(BF16)
    g_pad = jnp.concatenate([g, jnp.zeros((nl, d, LANES - N_GATES), w.dtype)], axis=2).astype(BF16)
    return main, g_pad, jnp.swapaxes(g, 1, 2).astype(BF16)


def kernel(x, c, ctx, c_ctx, w_ada, b_ada, norm1_w, norm2_w, w_in, b_gates, q_norm_w, k_norm_w, attn_sink,
           mlstm_norm_w, w_out, w_router, b_router, w_gate_up, b_gate_up, w_down, b_down):
    B, S, D = x.shape
    C = ctx.shape[1]
    depth = w_ada.shape[0]
    E = w_router.shape[-1]
    n_lat_rows = B * S
    R = n_lat_rows + B * C
    assert B < 16 and S % 1024 == 0 and C % 256 == 0 and (B * C) % 1024 == 0 and D % 128 == 0

    xall = jnp.concatenate([x.reshape(n_lat_rows, D), ctx.reshape(B * C, D)], axis=0)
    cv = jnp.zeros((16, D), F32).at[:B].set(c).at[B].set(c_ctx)
    mod = _ada_mod(cv, w_ada, b_ada)

    rope_tm = 512
    tabs = _rope_tables(S, rope_tm)
    reps = ATT_KV_W // ATT_HEAD_DIM
    gi = np.arange(ATT_KV_W) // ATT_HEAD_DIM
    bd = jnp.asarray(gi[:, None] == gi[None, :], BF16)

    w_main, w_g, w_gt = _permute_w_in(w_in)
    w_out_bf = w_out.astype(BF16)

    for l in range(depth):
        last = l == depth - 1
        mod3 = mod[l].reshape(16 * 6, 1, D)
        proj, g, gt = _inproj(xall, mod3, norm1_w[l], w_main, w_g, w_gt, l, n_lat_rows=n_lat_rows, seq=S, nbatch=B)
        q, kt = _qkrope(proj, tabs, jnp.tile(q_norm_w[l], reps).reshape(1, -1),
                        jnp.tile(k_norm_w[l], reps).reshape(1, -1), bd, n_lat_rows=n_lat_rows, seq=S)
        att = _attention(q, kt, proj, attn_sink[l], nbatch=B, seq=S, ctx_len=C, with_ctx_queries=not last)
        hf, hb = _mlstm(proj, g, gt, b_gates[l], nbatch=B, seq=S, ctx_len=C)

        T = n_lat_rows if last else R
        wr_t = w_router[l].T
        wr_hi = wr_t.astype(BF16)
        wr_lo = (wr_t - wr_hi.astype(F32)).astype(BF16)
        xmid, fx, lt = _outproj(att, hf, hb, proj, xall, mod3, mlstm_norm_w[l], norm2_w[l], w_out_bf,
                                wr_hi, wr_lo, b_router[l].reshape(E, 1), l,
                                n_rows=T, n_lat_rows=n_lat_rows, seq=S, nbatch=B)
        wts, pos, meta, code_init = _route(lt, tm=MOE_TM)
        code = _invert(pos.reshape(-1), code_init.reshape(-1))
        hs = _expert_up(meta, code, fx, w_gate_up, b_gate_up, l)
        yk = _expert_down(meta, code, hs, w_down, b_down, l, n_tok=T)
        xall = _combine(yk, xmid, wts.T, mod3, n_lat_rows=n_lat_rows, seq=S, nbatch=B)

    return xall[:n_lat_rows].reshape(B, S, D)
```
